```python
import jax, jax.numpy as jnp
from jax import lax
import numpy as np

D_MODEL = 1024
BATCH = 4
SEQ = 4096
DEPTH = 4
DEC_BATCH = 2
DEC_SEQ = 16384
PAST_LEN = 128

GRID_W = 64
PLE_DIM = 256
N_MIXERS = 3
QBLOCK = 128
ROPE_THETA = 10000.0
RMS_EPS = 1e-6
LN_EPS = 1e-5
DEEPNORM_ALPHA = (2 * DEPTH) ** 0.25
DEEPNORM_BETA = (8 * DEPTH) ** -0.25

A_HEADS = 16
A_Q_LORA = 384
A_KV_LORA = 256
A_NOPE = 64
A_ROPE = 32
A_V = 64
B_Q_HEADS = 16
B_KV_HEADS = 4
B_HEAD_DIM = 64
C_HEADS = 16
C_HEAD_DIM = 64
C_WIN_H = 8
C_WIN_W = 16
FF_DENSE = 2816
N_EXPERTS = 8
TOP_K = 2
FF_EXPERT = 3584

N_A = len(range(0, DEPTH, 3))
N_B = len(range(1, DEPTH, 3))
N_C = len(range(2, DEPTH, 3))
N_DENSE = len(range(0, DEPTH, 2))
N_MOE = len(range(1, DEPTH, 2))

kernel_name = 'hybrid_mla_gqa_natten_encoder'


def layer_norm(x, g, b):
    xf = x.astype(jnp.float32)
    mu = jnp.mean(xf, -1, keepdims=True)
    var = jnp.mean(jnp.square(xf - mu), -1, keepdims=True)
    return ((xf - mu) * lax.rsqrt(var + LN_EPS) * g.astype(jnp.float32) + b.astype(jnp.float32)).astype(x.dtype)


def rms_norm(x, g):
    xf = x.astype(jnp.float32)
    return (xf * lax.rsqrt(jnp.mean(xf * xf, -1, keepdims=True) + RMS_EPS) * g.astype(jnp.float32)).astype(x.dtype)


def rope_angles(pos, dim):
    inv = ROPE_THETA ** (-jnp.arange(0, dim, 2, dtype=jnp.float32) / dim)
    return pos.astype(jnp.float32)[:, None] * inv[None, :]


def rotate(x, ang):
    d2 = x.shape[-1] // 2
    x1 = x[..., :d2].astype(jnp.float32)
    x2 = x[..., d2:].astype(jnp.float32)
    c = jnp.cos(ang)[None, :, None, :]
    s = jnp.sin(ang)[None, :, None, :]
    return jnp.concatenate([x1 * c - x2 * s, x1 * s + x2 * c], -1).astype(x.dtype)


def axial_rotate(x, row_ang, col_ang):
    h = x.shape[-1] // 2
    return jnp.concatenate([rotate(x[..., :h], row_ang), rotate(x[..., h:], col_ang)], -1)


def sweep_attention(q, k, v, scale):
    B, S, H, Dk = q.shape
    Hk = k.shape[2]
    G = H // Hk
    qb = q.reshape(B, S // QBLOCK, QBLOCK, Hk, G, Dk).transpose(1, 0, 2, 3, 4, 5)

    def one_block(q_blk):
        s = jnp.einsum('bqhgd,bkhd->bhgqk', q_blk, k, preferred_element_type=jnp.float32) * scale
        pr = jax.nn.softmax(s, axis=-1)
        return jnp.einsum('bhgqk,bkhd->bqhgd', pr.astype(v.dtype), v)

    out = lax.map(one_block, qb)
    return out.transpose(1, 0, 2, 3, 4, 5).reshape(B, S, H, v.shape[-1])


def mla_mixer(x, w_dq, g_q, w_uq, w_dkv, g_kv, w_ukv, w_o):
    B, S, _ = x.shape
    ang = rope_angles(jnp.arange(S), A_ROPE)
    q = (rms_norm(x @ w_dq, g_q) @ w_uq).reshape(B, S, A_HEADS, A_NOPE + A_ROPE)
    q = jnp.concatenate([q[..., :A_NOPE], rotate(q[..., A_NOPE:], ang)], -1)
    ckv = x @ w_dkv
    kv = (rms_norm(ckv[..., :A_KV_LORA], g_kv) @ w_ukv).reshape(B, S, A_HEADS, A_NOPE + A_V)
    k_rope = rotate(ckv[..., A_KV_LORA:][:, :, None, :], ang)
    k = jnp.concatenate([kv[..., :A_NOPE], jnp.broadcast_to(k_rope, (B, S, A_HEADS, A_ROPE))], -1)
    v = kv[..., A_NOPE:]
    o = sweep_attention(q, k, v, (A_NOPE + A_ROPE) ** -0.5)
    return o.reshape(B, S, A_HEADS * A_V) @ w_o


def gqa_axial_mixer(x, w_qkv, g_q, g_k, w_o):
    B, S, _ = x.shape
    t = jnp.arange(S)
    row_ang = rope_angles(t // GRID_W, B_HEAD_DIM // 2)
    col_ang = rope_angles(t % GRID_W, B_HEAD_DIM // 2)
    nq = B_Q_HEADS * B_HEAD_DIM
    nk = B_KV_HEADS * B_HEAD_DIM
    qkv = x @ w_qkv
    q = qkv[..., :nq].reshape(B, S, B_Q_HEADS, B_HEAD_DIM)
    k = qkv[..., nq:nq + nk].reshape(B, S, B_KV_HEADS, B_HEAD_DIM)
    v = qkv[..., nq + nk:].reshape(B, S, B_KV_HEADS, B_HEAD_DIM)
    q = axial_rotate(rms_norm(q, g_q), row_ang, col_ang)
    k = axial_rotate(rms_norm(k, g_k), row_ang, col_ang)
    o = sweep_attention(q, k, v, B_HEAD_DIM ** -0.5)
    return o.reshape(B, S, nq) @ w_o


def neighbourhood_mixer(x, w_qkv, rpb, w_o):
    B, S, _ = x.shape
    rows = S // GRID_W
    kh = min(C_WIN_H, rows)
    qkv = (x @ w_qkv).reshape(B, rows, GRID_W, 3, C_HEADS, C_HEAD_DIM)
    q, k, v = qkv[:, :, :, 0], qkv[:, :, :, 1], qkv[:, :, :, 2]
    cols = jnp.arange(GRID_W)
    c_start = jnp.clip(cols - C_WIN_W // 2, 0, GRID_W - C_WIN_W)
    col_idx = c_start[:, None] + jnp.arange(C_WIN_W)[None, :]
    col_bias_idx = col_idx - cols[:, None] + (C_WIN_W - 1)
    scale = C_HEAD_DIM ** -0.5

    def one_row(r):
        r_start = jnp.clip(r - kh // 2, 0, rows - kh)
        k_nb = lax.dynamic_slice_in_dim(k, r_start, kh, axis=1)[:, :, col_idx]
        v_nb = lax.dynamic_slice_in_dim(v, r_start, kh, axis=1)[:, :, col_idx]
        q_row = lax.dynamic_index_in_dim(q, r, axis=1, keepdims=False)
        s = jnp.einsum('bqhd,biqjhd->bhqij', q_row, k_nb, preferred_element_type=jnp.float32) * scale
        row_bias_idx = r_start + jnp.arange(kh) - r + (C_WIN_H - 1)
        bias = rpb[:, row_bias_idx][:, :, col_bias_idx].transpose(0, 2, 1, 3)
        s = s + bias[None].astype(jnp.float32)
        pr = jax.nn.softmax(s.reshape(B, C_HEADS, GRID_W, kh * C_WIN_W), axis=-1)
        pr = pr.reshape(B, C_HEADS, GRID_W, kh, C_WIN_W).astype(v.dtype)
        return jnp.einsum('bhqij,biqjhd->bqhd', pr, v_nb)

    out = lax.map(one_row, jnp.arange(rows))
    out = out.transpose(1, 0, 2, 3, 4).reshape(B, S, C_HEADS * C_HEAD_DIM)
    return out @ w_o


def swiglu(x, w_gate, w_up, w_down):
    return (jax.nn.silu(x @ w_gate) * (x @ w_up)) @ w_down


def moe_swiglu(x, w_router, w_gate, w_up, w_down):
    B, S, D = x.shape
    xt = x.reshape(B * S, D)
    logits = (xt @ w_router).astype(jnp.float32)
    top_v, top_i = lax.top_k(logits, TOP_K)
    gates = jax.nn.softmax(top_v, axis=-1)
    combine = jnp.sum(jax.nn.one_hot(top_i, N_EXPERTS, dtype=jnp.float32) * gates[..., None], axis=1)
    combine = combine.astype(x.dtype)
    y = jnp.zeros_like(xt)
    for e in range(N_EXPERTS):
        y = y + combine[:, e:e + 1] * swiglu(xt, w_gate[e], w_up[e], w_down[e])
    return y.reshape(B, S, D)


def trunk(x, p, w):
    for i in range(DEPTH):
        kind = i % N_MIXERS
        j = i // N_MIXERS
        if kind == 0:
            h = mla_mixer(x, w['a_w_dq'][j], w['a_g_q'][j], w['a_w_uq'][j], w['a_w_dkv'][j],
                          w['a_g_kv'][j], w['a_w_ukv'][j], w['a_w_o'][j])
        elif kind == 1:
            h = gqa_axial_mixer(x, w['b_w_qkv'][j], w['b_g_q'][j], w['b_g_k'][j], w['b_w_o'][j])
        else:
            h = neighbourhood_mixer(x, w['c_w_qkv'][j], w['c_rpb'][j], w['c_w_o'][j])
        x = layer_norm(DEEPNORM_ALPHA * x + h, w['ln1_g'][i], w['ln1_b'][i])
        f_i = i // 2
        if i % 2 == 0:
            f = swiglu(x, w['f_w_gate'][f_i], w['f_w_up'][f_i], w['f_w_down'][f_i])
        else:
            f = moe_swiglu(x, w['m_w_router'][f_i], w['m_w_gate'][f_i], w['m_w_up'][f_i], w['m_w_down'][f_i])
        ple = jax.nn.sigmoid(x @ w['ple_w_gate'][i]) * (p[i] @ w['ple_w_in'][i])
        x = layer_norm(DEEPNORM_ALPHA * x + f + ple, w['ln2_g'][i], w['ln2_b'][i])
    return x


def setup_inputs(seed: int = 0) -> dict:
    key = jax.random.key(seed)
    ks = list(jax.random.split(key, 40))

    def nrm(shape, fan_in, scale=1.0):
        return jax.random.normal(ks.pop(), shape, jnp.float32) * (scale * fan_in ** -0.5)

    def gain(shape):
        return 1.0 + 0.01 * jax.random.normal(ks.pop(), shape, jnp.float32)

    def small(shape, s=0.01):
        return s * jax.random.normal(ks.pop(), shape, jnp.float32)

    D = D_MODEL
    beta = DEEPNORM_BETA
    return {
        'x_prompt': jax.random.normal(ks.pop(), (BATCH, SEQ, D), jnp.float32),
        'x_sample': jax.random.normal(ks.pop(), (DEC_BATCH, DEC_SEQ, D), jnp.float32),
        'p_prompt': jax.random.normal(ks.pop(), (DEPTH, BATCH, SEQ, PLE_DIM), jnp.float32),
        'p_sample': jax.random.normal(ks.pop(), (DEPTH, DEC_BATCH, DEC_SEQ, PLE_DIM), jnp.float32),
        'a_w_dq': nrm((N_A, D, A_Q_LORA), D),
        'a_g_q': gain((N_A, A_Q_LORA)),
        'a_w_uq': nrm((N_A, A_Q_LORA, A_HEADS * (A_NOPE + A_ROPE)), A_Q_LORA),
        'a_w_dkv': nrm((N_A, D, A_KV_LORA + A_ROPE), D),
        'a_g_kv': gain((N_A, A_KV_LORA)),
        'a_w_ukv': nrm((N_A, A_KV_LORA, A_HEADS * (A_NOPE + A_V)), A_KV_LORA),
        'a_w_o': nrm((N_A, A_HEADS * A_V, D), A_HEADS * A_V, beta),
        'b_w_qkv': nrm((N_B, D, (B_Q_HEADS + 2 * B_KV_HEADS) * B_HEAD_DIM), D),
        'b_g_q': gain((N_B, B_HEAD_DIM)),
        'b_g_k': gain((N_B, B_HEAD_DIM)),
        'b_w_o': nrm((N_B, B_Q_HEADS * B_HEAD_DIM, D), B_Q_HEADS * B_HEAD_DIM, beta),
        'c_w_qkv': nrm((N_C, D, 3 * C_HEADS * C_HEAD_DIM), D),
        'c_rpb': small((N_C, C_HEADS, 2 * C_WIN_H - 1, 2 * C_WIN_W - 1), 0.1),
        'c_w_o': nrm((N_C, C_HEADS * C_HEAD_DIM, D), C_HEADS * C_HEAD_DIM, beta),
        'ln1_g': gain((DEPTH, D)),
        'ln1_b': small((DEPTH, D)),
        'ln2_g': gain((DEPTH, D)),
        'ln2_b': small((DEPTH, D)),
        'f_w_gate': nrm((N_DENSE, D, FF_DENSE), D),
        'f_w_up': nrm((N_DENSE, D, FF_DENSE), D),
        'f_w_down': nrm((N_DENSE, FF_DENSE, D), FF_DENSE, beta),
        'm_w_router': nrm((N_MOE, D, N_EXPERTS), D),
        'm_w_gate': nrm((N_MOE, N_EXPERTS, D, FF_EXPERT), D),
        'm_w_up': nrm((N_MOE, N_EXPERTS, D, FF_EXPERT), D),
        'm_w_down': nrm((N_MOE, N_EXPERTS, FF_EXPERT, D), FF_EXPERT, beta),
        'ple_w_gate': nrm((DEPTH, D, D), D),
        'ple_w_in': nrm((DEPTH, PLE_DIM, D), PLE_DIM, beta),
    }


def reference(x_prompt, x_sample, p_prompt, p_sample,
              a_w_dq, a_g_q, a_w_uq, a_w_dkv, a_g_kv, a_w_ukv, a_w_o,
              b_w_qkv, b_g_q, b_g_k, b_w_o,
              c_w_qkv, c_rpb, c_w_o,
              ln1_g, ln1_b, ln2_g, ln2_b,
              f_w_gate, f_w_up, f_w_down,
              m_w_router, m_w_gate, m_w_up, m_w_down,
              ple_w_gate, ple_w_in):
    w = dict(a_w_dq=a_w_dq, a_g_q=a_g_q, a_w_uq=a_w_uq, a_w_dkv=a_w_dkv, a_g_kv=a_g_kv,
             a_w_ukv=a_w_ukv, a_w_o=a_w_o,
             b_w_qkv=b_w_qkv, b_g_q=b_g_q, b_g_k=b_g_k, b_w_o=b_w_o,
             c_w_qkv=c_w_qkv, c_rpb=c_rpb, c_w_o=c_w_o,
             ln1_g=ln1_g, ln1_b=ln1_b, ln2_g=ln2_g, ln2_b=ln2_b,
             f_w_gate=f_w_gate, f_w_up=f_w_up, f_w_down=f_w_down,
             m_w_router=m_w_router, m_w_gate=m_w_gate, m_w_up=m_w_up, m_w_down=m_w_down,
             ple_w_gate=ple_w_gate, ple_w_in=ple_w_in)
    y_prompt = trunk(x_prompt, p_prompt, w)
    y_sample = trunk(x_sample, p_sample, w)
    return (y_prompt, y_sample)
```

```python
import functools

import jax
import jax.numpy as jnp
from jax import lax
from jax.experimental import pallas as pl
from jax.experimental.pallas import tpu as pltpu

F32 = jnp.float32
BF16 = jnp.bfloat16

D_MODEL = 1024
DEPTH = 4
GRID_W = 64
PLE_DIM = 256
N_MIXERS = 3
ROPE_THETA = 10000.0
RMS_EPS = 1e-6
LN_EPS = 1e-5
DEEPNORM_ALPHA = (2 * DEPTH) ** 0.25

A_HEADS = 16
A_Q_LORA = 384
A_KV_LORA = 256
A_NOPE = 64
A_ROPE = 32
A_V = 64
A_DK = 128
B_Q_HEADS = 16
B_KV_HEADS = 4
B_HEAD_DIM = 64
C_HEADS = 16
C_HEAD_DIM = 64
C_WIN_H = 8
C_WIN_W = 16
FF_DENSE = 2816
N_EXPERTS = 8
TOP_K = 2
FF_EXPERT = 3584

V_ROWS = 80
NEG = -1e30
VMEM_LIMIT = 56 * 1024 * 1024
ROW_BLOCK = 512
KV_CHUNK = 512
MOE_TILE = 512
NT = (((1,), (1,)), ((), ()))


def _params(*sem):
    return pltpu.CompilerParams(dimension_semantics=sem, vmem_limit_bytes=VMEM_LIMIT)


def _full(shape):
    return pl.BlockSpec(shape, lambda *_: (0,) * len(shape))


def _rms(x, g):
    return x * lax.rsqrt(jnp.mean(x * x, axis=-1, keepdims=True) + RMS_EPS) * g


def _layer_norm(x, g, b):
    mu = jnp.mean(x, axis=-1, keepdims=True)
    xc = x - mu
    var = jnp.mean(xc * xc, axis=-1, keepdims=True)
    return xc * lax.rsqrt(var + LN_EPS) * g + b


def _tile_lanes(t, n):
    return jnp.concatenate([t] * n, axis=-1)


def _mla_down_kernel(x_ref, w_ref, gq_ref, gkv_ref, cos_ref, sin_ref, cq_ref, ckv_ref, kr_ref):
    acc = jnp.dot(x_ref[...].astype(BF16), w_ref[...], preferred_element_type=F32)
    cq_ref[...] = _rms(acc[:, :A_Q_LORA], gq_ref[...]).astype(BF16)
    ckv_ref[...] = _rms(acc[:, A_Q_LORA:640], gkv_ref[...]).astype(BF16)
    kr_ref[...] = (acc[:, 640:768] * cos_ref[...] + acc[:, 768:896] * sin_ref[...]).astype(BF16)


def _mla_down(x, w, gq, gkv, cos_t, sin_t, S, bm):
    N = x.shape[0]
    nsb = S // bm
    row = lambda i: (i, 0)
    tab = lambda i: (i % nsb, 0)
    return pl.pallas_call(
        _mla_down_kernel,
        grid=(N // bm,),
        in_specs=[pl.BlockSpec((bm, D_MODEL), row), _full(w.shape), _full(gq.shape), _full(gkv.shape),
                  pl.BlockSpec((bm, 128), tab), pl.BlockSpec((bm, 128), tab)],
        out_specs=[pl.BlockSpec((bm, A_Q_LORA), row), pl.BlockSpec((bm, A_KV_LORA), row),
                   pl.BlockSpec((bm, 128), row)],
        out_shape=[jax.ShapeDtypeStruct((N, A_Q_LORA), BF16), jax.ShapeDtypeStruct((N, A_KV_LORA), BF16),
                   jax.ShapeDtypeStruct((N, 128), BF16)],
        compiler_params=_params("parallel"),
        name="mla_down",
    )(x, w, gq, gkv, cos_t, sin_t)


def _mla_q_up_kernel(cq_ref, w1_ref, w2_ref, cos_ref, sin_ref, q_ref):
    cq = cq_ref[...]
    a1 = jnp.dot(cq, w1_ref[...], preferred_element_type=F32)
    a2 = jnp.dot(cq, w2_ref[...], preferred_element_type=F32)
    q = a1 * _tile_lanes(cos_ref[...], A_HEADS) + a2 * _tile_lanes(sin_ref[...], A_HEADS)
    for h in range(A_HEADS):
        q_ref[0, h] = q[:, h * A_DK:(h + 1) * A_DK].astype(BF16)


def _mla_q_up(cq, w1, w2, cos_t, sin_t, B, S, bm):
    nsb = S // bm
    return pl.pallas_call(
        _mla_q_up_kernel,
        grid=(B * nsb,),
        in_specs=[pl.BlockSpec((bm, A_Q_LORA), lambda i: (i, 0)), _full(w1.shape), _full(w2.shape),
                  pl.BlockSpec((bm, 128), lambda i: (i % nsb, 0)), pl.BlockSpec((bm, 128), lambda i: (i % nsb, 0))],
        out_specs=pl.BlockSpec((1, A_HEADS, bm, A_DK), lambda i: (i // nsb, 0, i % nsb, 0)),
        out_shape=jax.ShapeDtypeStruct((B, A_HEADS, S, A_DK), BF16),
        compiler_params=_params("parallel"),
        name="mla_q_up",
    )(cq, w1, w2, cos_t, sin_t)


def _ones_rows(n):
    r = lax.broadcasted_iota(jnp.int32, (V_ROWS - 64, n), 0)
    return jnp.where(r == 0, 1.0, 0.0).astype(BF16)


def _mla_kv_up_kernel(ckv_ref, kr_ref, wk_ref, wvt_ref, k_ref, vt_ref):
    ckv = ckv_ref[...]
    ak = jnp.dot(ckv, wk_ref[...], preferred_element_type=F32)
    kr = kr_ref[...].astype(F32)
    avt = lax.dot_general(wvt_ref[...], ckv, NT, preferred_element_type=F32)
    ones = _ones_rows(ckv.shape[0])
    for h in range(A_HEADS):
        k_ref[0, h] = (ak[:, h * A_DK:(h + 1) * A_DK] + kr).astype(BF16)
        vt_ref[0, h, 0, 0:A_V, :] = avt[h * A_V:(h + 1) * A_V, :].astype(BF16)
        vt_ref[0, h, 0, A_V:V_ROWS, :] = ones


def _mla_kv_up(ckv, kr, wk, wvt, B, S, bm):
    nsb = S // bm
    return pl.pallas_call(
        _mla_kv_up_kernel,
        grid=(B * nsb,),
        in_specs=[pl.BlockSpec((bm, A_KV_LORA), lambda i: (i, 0)), pl.BlockSpec((bm, 128), lambda i: (i, 0)),
                  _full(wk.shape), _full(wvt.shape)],
        out_specs=[pl.BlockSpec((1, A_HEADS, bm, A_DK), lambda i: (i // nsb, 0, i % nsb, 0)),
                   pl.BlockSpec((1, A_HEADS, 1, V_ROWS, bm), lambda i: (i // nsb, 0, i % nsb, 0, 0))],
        out_shape=[jax.ShapeDtypeStruct((B, A_HEADS, S, A_DK), BF16),
                   jax.ShapeDtypeStruct((B, A_HEADS, nsb, V_ROWS, bm), BF16)],
        compiler_params=_params("parallel"),
        name="mla_kv_up",
    )(ckv, kr, wk, wvt)


def _gqa_proj_kernel(x_ref, wq_ref, wqp_ref, wk_ref, wvt_ref, ones_ref, gq_ref, gqp_ref, gk_ref, gkp_ref,
                     cos_ref, sin_ref, q_ref, k_ref, vt_ref):
    xb = x_ref[...].astype(BF16)
    nq = B_Q_HEADS * B_HEAD_DIM
    nk = B_KV_HEADS * B_HEAD_DIM
    cos = cos_ref[...]
    sin = sin_ref[...]
    ones_bd = ones_ref[...]

    def norm_rope(a, ap, g, gp, n):
        ss = jnp.dot((a * a).astype(BF16), ones_bd[:n, :n], preferred_element_type=F32) * (1.0 / B_HEAD_DIM)
        r = lax.rsqrt(ss + RMS_EPS)
        reps = n // 128
        return r * (a * g * _tile_lanes(cos, reps) + ap * gp * _tile_lanes(sin, reps))

    aq = jnp.dot(xb, wq_ref[...], preferred_element_type=F32)
    aqp = jnp.dot(xb, wqp_ref[...], preferred_element_type=F32)
    q = norm_rope(aq, aqp, gq_ref[...], gqp_ref[...], nq) * (B_HEAD_DIM ** -0.5)
    akk = jnp.dot(xb, wk_ref[...], preferred_element_type=F32)
    k = norm_rope(akk[:, :nk], akk[:, nk:], gk_ref[...], gkp_ref[...], nk)
    avt = lax.dot_general(wvt_ref[...], xb, NT, preferred_element_type=F32)
    ones = _ones_rows(xb.shape[0])
    for h in range(B_Q_HEADS):
        q_ref[0, h] = q[:, h * B_HEAD_DIM:(h + 1) * B_HEAD_DIM].astype(BF16)
    for h in range(B_KV_HEADS):
        k_ref[0, h] = k[:, h * B_HEAD_DIM:(h + 1) * B_HEAD_DIM].astype(BF16)
        vt_ref[0, h, 0, 0:B_HEAD_DIM, :] = avt[h * B_HEAD_DIM:(h + 1) * B_HEAD_DIM, :].astype(BF16)
        vt_ref[0, h, 0, B_HEAD_DIM:V_ROWS, :] = ones


def _gqa_proj(x, wq, wqp, wk, wvt, ones_bd, gq, gqp, gk, gkp, cos_t, sin_t, B, S, bm):
    nsb = S // bm
    hm = lambda i: (i // nsb, 0, i % nsb, 0)
    tab = lambda i: (i % nsb, 0)
    ins = [x, wq, wqp, wk, wvt, ones_bd, gq, gqp, gk, gkp, cos_t, sin_t]
    specs = [pl.BlockSpec((bm, D_MODEL), lambda i: (i, 0))] + [_full(a.shape) for a in ins[1:10]]
    specs += [pl.BlockSpec((bm, 128), tab), pl.BlockSpec((bm, 128), tab)]
    return pl.pallas_call(
        _gqa_proj_kernel,
        grid=(B * nsb,),
        in_specs=specs,
        out_specs=[pl.BlockSpec((1, B_Q_HEADS, bm, B_HEAD_DIM), hm), pl.BlockSpec((1, B_KV_HEADS, bm, B_HEAD_DIM), hm),
                   pl.BlockSpec((1, B_KV_HEADS, 1, V_ROWS, bm), lambda i: (i // nsb, 0, i % nsb, 0, 0))],
        out_shape=[jax.ShapeDtypeStruct((B, B_Q_HEADS, S, B_HEAD_DIM), BF16),
                   jax.ShapeDtypeStruct((B, B_KV_HEADS, S, B_HEAD_DIM), BF16),
                   jax.ShapeDtypeStruct((B, B_KV_HEADS, nsb, V_ROWS, bm), BF16)],
        compiler_params=_params("parallel"),
        name="gqa_proj",
    )(*ins)


def _flash_kernel(q_ref, k_ref, vt_ref, o_ref, *, hp, G, tq, tk, nk, dv):
    TQ = G * tq
    dk = q_ref.shape[-1]
    outs = []
    for h in range(hp):
        q = q_ref[0, h * G:(h + 1) * G].reshape(TQ, dk)

        def body(j, carry, h=h, q=q):
            m, acc = carry
            off = pl.multiple_of(j * tk, tk)
            k = k_ref[0, h, pl.ds(off, tk), :]
            s = lax.dot_general(k, q, NT, preferred_element_type=F32)
            m_new = jnp.maximum(m, jnp.max(s, axis=0, keepdims=True))
            alpha = jnp.exp(m - m_new)
            p = jnp.exp(s - m_new).astype(BF16)
            acc = alpha * acc + jnp.dot(vt_ref[0, h, j], p, preferred_element_type=F32)
            return m_new, acc

        m0 = jnp.full((1, TQ), NEG, F32)
        acc0 = jnp.zeros((V_ROWS, TQ), F32)
        _, acc = lax.fori_loop(0, nk, body, (m0, acc0))
        o = acc[:dv] * (1.0 / acc[dv:dv + 1])
        for g in range(G):
            outs.append(o[:, g * tq:(g + 1) * tq])
    o_ref[0] = jnp.concatenate(outs, axis=0).T.astype(o_ref.dtype)


def _flash(q, k, vt, *, hp, G, tq, dv=64):
    B, H, S, dk = q.shape
    Hk = k.shape[1]
    nk, tk = vt.shape[2], vt.shape[4]
    kern = functools.partial(_flash_kernel, hp=hp, G=G, tq=tq, tk=tk, nk=nk, dv=dv)
    return pl.pallas_call(
        kern,
        grid=(B, Hk // hp, S // tq),
        in_specs=[pl.BlockSpec((1, hp * G, tq, dk), lambda b, h, i: (b, h, i, 0)),
                  pl.BlockSpec((1, hp, S, dk), lambda b, h, i: (b, h, 0, 0)),
                  pl.BlockSpec((1, hp, nk, V_ROWS, tk), lambda b, h, i: (b, h, 0, 0, 0))],
        out_specs=pl.BlockSpec((1, tq, hp * G * dv), lambda b, h, i: (b, i, h)),
        out_shape=jax.ShapeDtypeStruct((B, S, H * dv), BF16),
        compiler_params=_params("parallel", "parallel", "arbitrary"),
        name="flash_attention",
    )(q, k, vt)


def _nbr_proj_kernel(x_ref, w_ref, q_ref, k_ref, v_ref):
    acc = jnp.dot(x_ref[...].astype(BF16), w_ref[...], preferred_element_type=F32)
    n = C_HEADS * C_HEAD_DIM
    for h in range(C_HEADS):
        lo = h * C_HEAD_DIM
        q_ref[0, h] = (acc[:, lo:lo + C_HEAD_DIM] * (C_HEAD_DIM ** -0.5)).astype(BF16)
        k_ref[0, h] = acc[:, n + lo:n + lo + C_HEAD_DIM].astype(BF16)
        v_ref[0, h] = acc[:, 2 * n + lo:2 * n + lo + C_HEAD_DIM].astype(BF16)


def _nbr_proj(x, w, B, S, bm):
    nsb = S // bm
    hm = lambda i: (i // nsb, 0, i % nsb, 0)
    shp = jax.ShapeDtypeStruct((B, C_HEADS, S, C_HEAD_DIM), BF16)
    return pl.pallas_call(
        _nbr_proj_kernel,
        grid=(B * nsb,),
        in_specs=[pl.BlockSpec((bm, D_MODEL), lambda i: (i, 0)), _full(w.shape)],
        out_specs=[pl.BlockSpec((1, C_HEADS, bm, C_HEAD_DIM), hm)] * 3,
        out_shape=[shp] * 3,
        compiler_params=_params("parallel"),
        name="nbr_proj",
    )(x, w)


def _nbr_kernel(*refs):
    q_ref = refs[0]
    k_refs = refs[1:1 + C_WIN_H]
    v_refs = refs[1 + C_WIN_H:1 + 2 * C_WIN_H]
    bias_ref = refs[1 + 2 * C_WIN_H]
    o_ref = refs[2 + 2 * C_WIN_H]
    outs = []
    for h in range(C_HEADS):
        k = jnp.concatenate([r[0, h] for r in k_refs], axis=0)
        v = jnp.concatenate([r[0, h] for r in v_refs], axis=0)
        s = lax.dot_general(q_ref[0, h], k, NT, preferred_element_type=F32) + bias_ref[0, h]
        m = jnp.max(s, axis=-1, keepdims=True)
        p = jnp.exp(s - m)
        l = jnp.sum(p, axis=-1, keepdims=True)
        outs.append(jnp.dot(p.astype(BF16), v, preferred_element_type=F32) / l)
    o_ref[0] = jnp.concatenate(outs, axis=-1).astype(o_ref.dtype)


def _nbr_attention(q, k, v, bias):
    B, H, S, d = q.shape
    rows = S // GRID_W
    kh = C_WIN_H

    def r_start(r):
        return jnp.clip(r - kh // 2, 0, rows - kh)

    blk = (1, H, GRID_W, d)
    kv_specs = [pl.BlockSpec(blk, functools.partial(lambda b, r, i: (b, 0, r_start(r) + i, 0), i=i)) for i in range(kh)]
    return pl.pallas_call(
        _nbr_kernel,
        grid=(B, rows),
        in_specs=[pl.BlockSpec(blk, lambda b, r: (b, 0, r, 0))] + kv_specs + kv_specs
        + [pl.BlockSpec((1, H, GRID_W, kh * GRID_W), lambda b, r: (r - r_start(r), 0, 0, 0))],
        out_specs=pl.BlockSpec((1, GRID_W, H * d), lambda b, r: (b, r, 0)),
        out_shape=jax.ShapeDtypeStruct((B, S, H * d), BF16),
        compiler_params=_params("parallel", "arbitrary"),
        name="nbr_attention",
    )(q, *([k] * kh), *([v] * kh), bias)


def _out_ln_kernel(a_ref, w_ref, x_ref, g_ref, b_ref, o_ref):
    h = jnp.dot(a_ref[...], w_ref[...], preferred_element_type=F32)
    o_ref[...] = _layer_norm(DEEPNORM_ALPHA * x_ref[...] + h, g_ref[...], b_ref[...])


def _out_ln(a, w, x, g, b, bm):
    N = x.shape[0]
    row = lambda i: (i, 0)
    return pl.pallas_call(
        _out_ln_kernel,
        grid=(N // bm,),
        in_specs=[pl.BlockSpec((bm, a.shape[1]), row), _full(w.shape), pl.BlockSpec((bm, D_MODEL), row),
                  _full(g.shape), _full(b.shape)],
        out_specs=pl.BlockSpec((bm, D_MODEL), row),
        out_shape=jax.ShapeDtypeStruct((N, D_MODEL), F32),
        compiler_params=_params("parallel"),
        name="out_proj_ln",
    )(a, w, x, g, b)


def _swiglu_up_kernel(x_ref, wg_ref, wu_ref, h_ref):
    xb = x_ref[...].astype(BF16)
    g = jnp.dot(xb, wg_ref[...], preferred_element_type=F32)
    u = jnp.dot(xb, wu_ref[...], preferred_element_type=F32)
    h_ref[...] = (g * jax.nn.sigmoid(g) * u).astype(BF16)


def _swiglu_up(x, wg, wu, bm, bn):
    N = x.shape[0]
    F = wg.shape[1]
    return pl.pallas_call(
        _swiglu_up_kernel,
        grid=(F // bn, N // bm),
        in_specs=[pl.BlockSpec((bm, D_MODEL), lambda j, i: (i, 0)), pl.BlockSpec((D_MODEL, bn), lambda j, i: (0, j)),
                  pl.BlockSpec((D_MODEL, bn), lambda j, i: (0, j))],
        out_specs=pl.BlockSpec((bm, bn), lambda j, i: (i, j)),
        out_shape=jax.ShapeDtypeStruct((N, F), BF16),
        compiler_params=_params("parallel", "parallel"),
        name="swiglu_up",
    )(x, wg, wu)


def _ple(x, p_ref, wpg_ref, wpi_ref):
    gate = jax.nn.sigmoid(jnp.dot(x.astype(BF16), wpg_ref[...], preferred_element_type=F32))
    return gate * jnp.dot(p_ref[...].astype(BF16), wpi_ref[...], preferred_element_type=F32)


def _ffn_down_ln_kernel(h_ref, wd_ref, x_ref, p_ref, wpg_ref, wpi_ref, g_ref, b_ref, o_ref):
    x = x_ref[...]
    f = jnp.dot(h_ref[...], wd_ref[...], preferred_element_type=F32)
    o_ref[...] = _layer_norm(DEEPNORM_ALPHA * x + f + _ple(x, p_ref, wpg_ref, wpi_ref), g_ref[...], b_ref[...])


def _ffn_down_ln(h, wd, x, p, wpg, wpi, g, b, bm):
    N = x.shape[0]
    row = lambda i: (i, 0)
    return pl.pallas_call(
        _ffn_down_ln_kernel,
        grid=(N // bm,),
        in_specs=[pl.BlockSpec((bm, h.shape[1]), row), _full(wd.shape), pl.BlockSpec((bm, D_MODEL), row),
                  pl.BlockSpec((bm, PLE_DIM), row), _full(wpg.shape), _full(wpi.shape), _full(g.shape), _full(b.shape)],
        out_specs=pl.BlockSpec((bm, D_MODEL), row),
        out_shape=jax.ShapeDtypeStruct((N, D_MODEL), F32),
        compiler_params=_params("parallel"),
        name="ffn_down_ple_ln",
    )(h, wd, x, p, wpg, wpi, g, b)


def _router_kernel(x_ref, w_ref, idx_ref, gate_ref):
    logits = jnp.dot(x_ref[...], w_ref[...], preferred_element_type=F32, precision=lax.Precision.HIGHEST)
    lane = lax.broadcasted_iota(jnp.int32, logits.shape, 1).astype(F32)
    logits = jnp.where(lane < N_EXPERTS, logits, NEG)
    m1 = jnp.max(logits, axis=-1, keepdims=True)
    i1 = jnp.min(jnp.where(logits == m1, lane, 128.0), axis=-1, keepdims=True)
    rest = jnp.where(lane == i1, NEG, logits)
    m2 = jnp.max(rest, axis=-1, keepdims=True)
    i2 = jnp.min(jnp.where(rest == m2, lane, 128.0), axis=-1, keepdims=True)
    e = jnp.exp(m2 - m1)
    g1 = 1.0 / (1.0 + e)
    idx_ref[...] = jnp.where(lane == 0.0, i1, jnp.where(lane == 1.0, i2, 0.0)).astype(jnp.int32)
    gate_ref[...] = jnp.where(lane == 0.0, g1, jnp.where(lane == 1.0, e * g1, 0.0))


def _router(x, w, bm):
    N = x.shape[0]
    row = lambda i: (i, 0)
    return pl.pallas_call(
        _router_kernel,
        grid=(N // bm,),
        in_specs=[pl.BlockSpec((bm, D_MODEL), row), _full(w.shape)],
        out_specs=[pl.BlockSpec((bm, 128), row)] * 2,
        out_shape=[jax.ShapeDtypeStruct((N, 128), jnp.int32), jax.ShapeDtypeStruct((N, 128), F32)],
        compiler_params=_params("parallel"),
        name="moe_router",
    )(x, w)


def _moe_ffn_kernel(te_ref, nt_ref, xs_ref, wg_ref, wu_ref, wd_ref, o_ref, acc_ref):
    t = pl.program_id(0)
    j = pl.program_id(1)

    @pl.when(t < nt_ref[0])
    def _():
        xs = xs_ref[...]
        g = jnp.dot(xs, wg_ref[0], preferred_element_type=F32)
        u = jnp.dot(xs, wu_ref[0], preferred_element_type=F32)
        h = (g * jax.nn.sigmoid(g) * u).astype(BF16)
        y = jnp.dot(h, wd_ref[0], preferred_element_type=F32)

        @pl.when(j == 0)
        def _():
            acc_ref[...] = y

        @pl.when(j == pl.num_programs(1) - 1)
        def _():
            o_ref[...] = acc_ref[...] + y

    @pl.when(t >= nt_ref[0])
    def _():
        o_ref[...] = jnp.zeros_like(o_ref)


def _moe_ffn(tile_expert, n_tiles, xs, wg, wu, wd, tm, halves=2):
    P = xs.shape[0]
    fh = FF_EXPERT // halves
    grid_spec = pltpu.PrefetchScalarGridSpec(
        num_scalar_prefetch=2,
        grid=(P // tm, halves),
        in_specs=[pl.BlockSpec((tm, D_MODEL), lambda t, j, te, nt: (t, 0)),
                  pl.BlockSpec((1, D_MODEL, fh), lambda t, j, te, nt: (te[t], 0, j)),
                  pl.BlockSpec((1, D_MODEL, fh), lambda t, j, te, nt: (te[t], 0, j)),
                  pl.BlockSpec((1, fh, D_MODEL), lambda t, j, te, nt: (te[t], j, 0))],
        out_specs=pl.BlockSpec((tm, D_MODEL), lambda t, j, te, nt: (t, 0)),
        scratch_shapes=[pltpu.VMEM((tm, D_MODEL), F32)],
    )
    return pl.pallas_call(
        _moe_ffn_kernel,
        grid_spec=grid_spec,
        out_shape=jax.ShapeDtypeStruct((P, D_MODEL), F32),
        compiler_params=_params("arbitrary", "arbitrary"),
        name="moe_expert_ffn",
    )(tile_expert, n_tiles, xs, wg, wu, wd)


def _moe_combine_ln_kernel(ya_ref, yb_ref, gate_ref, x_ref, p_ref, wpg_ref, wpi_ref, g_ref, b_ref, o_ref):
    x = x_ref[...]
    gate = gate_ref[...]
    f = gate[:, 0:1] * ya_ref[...] + gate[:, 1:2] * yb_ref[...]
    o_ref[...] = _layer_norm(DEEPNORM_ALPHA * x + f + _ple(x, p_ref, wpg_ref, wpi_ref), g_ref[...], b_ref[...])


def _moe_combine_ln(ya, yb, gates, x, p, wpg, wpi, g, b, bm):
    N = x.shape[0]
    row = lambda i: (i, 0)
    rows = pl.BlockSpec((bm, D_MODEL), row)
    return pl.pallas_call(
        _moe_combine_ln_kernel,
        grid=(N // bm,),
        in_specs=[rows, rows, pl.BlockSpec((bm, 128), row), rows, pl.BlockSpec((bm, PLE_DIM), row),
                  _full(wpg.shape), _full(wpi.shape), _full(g.shape), _full(b.shape)],
        out_specs=rows,
        out_shape=jax.ShapeDtypeStruct((N, D_MODEL), F32),
        compiler_params=_params("parallel"),
        name="moe_combine_ple_ln",
    )(ya, yb, gates, x, p, wpg, wpi, g, b)


def _route(idx, tm):
    N = idx.shape[0]
    e_flat = idx.reshape(-1)
    onehot = (e_flat[:, None] == jnp.arange(N_EXPERTS, dtype=jnp.int32)[None, :]).astype(jnp.int32)
    csum = jnp.cumsum(onehot, axis=0)
    rank = jnp.sum(csum * onehot, axis=1) - 1
    counts = csum[-1]
    padded = ((counts + tm - 1) // tm) * tm
    ends = jnp.cumsum(padded)
    slot = (ends - padded)[e_flat] + rank
    P = TOP_K * N + N_EXPERTS * tm
    row_token = jnp.zeros((P,), jnp.int32).at[slot].set(jnp.arange(TOP_K * N, dtype=jnp.int32) // TOP_K)
    tile_start = jnp.arange(P // tm, dtype=jnp.int32) * tm
    tile_expert = jnp.minimum(jnp.sum((tile_start[:, None] >= ends[None, :]).astype(jnp.int32), axis=1), N_EXPERTS - 1)
    n_tiles = (ends[-1] // tm).astype(jnp.int32).reshape(1)
    return slot.reshape(N, TOP_K), row_token, tile_expert.astype(jnp.int32), n_tiles


def _swap_halves(w, group):
    shp = w.shape
    w = w.reshape(shp[:-1] + (shp[-1] // group, 2, group // 2))
    return jnp.flip(w, axis=-2).reshape(shp)


def _rope_tables(pos, dim):
    inv = ROPE_THETA ** (-jnp.arange(0, dim, 2, dtype=F32) / dim)
    ang = pos.astype(F32)[:, None] * inv[None, :]
    c, s = jnp.cos(ang), jnp.sin(ang)
    return jnp.concatenate([c, c], -1), jnp.concatenate([-s, s], -1)


def _mla_prep(w_dq, w_dkv, w_uq, w_ukv, S):
    wd = jnp.zeros((D_MODEL, 896), F32)
    wd = wd.at[:, :A_Q_LORA].set(w_dq).at[:, A_Q_LORA:640].set(w_dkv[:, :A_KV_LORA])
    w_r = w_dkv[:, A_KV_LORA:]
    wd = wd.at[:, 640 + A_NOPE:640 + A_NOPE + A_ROPE].set(w_r)
    wd = wd.at[:, 768 + A_NOPE:768 + A_NOPE + A_ROPE].set(_swap_halves(w_r, A_ROPE))
    uq = w_uq.reshape(A_Q_LORA, A_HEADS, A_NOPE + A_ROPE)
    w1 = jnp.pad(uq, ((0, 0), (0, 0), (0, A_DK - A_NOPE - A_ROPE))).reshape(A_Q_LORA, A_HEADS * A_DK)
    w2 = jnp.pad(_swap_halves(uq[..., A_NOPE:], A_ROPE), ((0, 0), (0, 0), (A_NOPE, A_DK - A_NOPE - A_ROPE)))
    w2 = w2.reshape(A_Q_LORA, A_HEADS * A_DK)
    ukv = w_ukv.reshape(A_KV_LORA, A_HEADS, A_NOPE + A_V)
    wk = jnp.pad(ukv[..., :A_NOPE], ((0, 0), (0, 0), (0, A_DK - A_NOPE))).reshape(A_KV_LORA, A_HEADS * A_DK)
    wvt = ukv[..., A_NOPE:].reshape(A_KV_LORA, A_HEADS * A_V).T
    c, s = _rope_tables(jnp.arange(S), A_ROPE)
    pad = lambda t, fill: jnp.concatenate([jnp.full((S, A_NOPE), fill, F32), t, jnp.zeros((S, A_DK - A_NOPE - A_ROPE), F32)], -1)
    scale = (A_NOPE + A_ROPE) ** -0.5
    tabs = dict(k_cos=pad(c, 0.0), k_sin=pad(s, 0.0), q_cos=pad(c, 1.0) * scale, q_sin=pad(s, 0.0) * scale)
    return wd.astype(BF16), w1.astype(BF16), w2.astype(BF16), wk.astype(BF16), wvt.astype(BF16), tabs


def _gqa_prep(w_qkv, g_q, g_k, S):
    nq = B_Q_HEADS * B_HEAD_DIM
    nk = B_KV_HEADS * B_HEAD_DIM
    half = B_HEAD_DIM // 2
    wq, wk, wv = w_qkv[:, :nq], w_qkv[:, nq:nq + nk], w_qkv[:, nq + nk:]
    wkk = jnp.concatenate([wk, _swap_halves(wk, half)], axis=1)
    t = jnp.arange(S)
    cr, sr = _rope_tables(t // GRID_W, half)
    cc, sc = _rope_tables(t % GRID_W, half)
    cos_t = jnp.tile(jnp.concatenate([cr, cc], -1), (1, 2))
    sin_t = jnp.tile(jnp.concatenate([sr, sc], -1), (1, 2))
    lane = jnp.arange(nq) // B_HEAD_DIM
    ones_bd = (lane[:, None] == lane[None, :]).astype(BF16)
    gq = jnp.tile(g_q, B_Q_HEADS)[None, :]
    gk = jnp.tile(g_k, B_KV_HEADS)[None, :]
    return (wq.astype(BF16), _swap_halves(wq, half).astype(BF16), wkk.astype(BF16), wv.T.astype(BF16), ones_bd,
            gq, _swap_halves(gq, half), gk, _swap_halves(gk, half), cos_t, sin_t)


def _nbr_bias(rpb):
    kh, kw, W = C_WIN_H, C_WIN_W, GRID_W
    qc = jnp.arange(W)[:, None]
    kc = jnp.arange(W)[None, :]
    c_start = jnp.clip(qc - kw // 2, 0, W - kw)
    inside = (kc >= c_start) & (kc < c_start + kw)
    col_idx = jnp.clip(kc - qc + (kw - 1), 0, 2 * kw - 2)
    v = jnp.arange(kh)[:, None]
    i = jnp.arange(kh)[None, :]
    row_idx = i - v + (kh - 1)
    b = rpb[:, row_idx][:, :, :, col_idx]
    b = jnp.where(inside[None, None, None], b, NEG)
    return b.transpose(1, 0, 3, 2, 4).reshape(kh, C_HEADS, W, kh * W).astype(F32)


def _trunk(x, p, w):
    B, S, _ = x.shape
    N = B * S
    bm = min(ROW_BLOCK, S)
    tq = min(512, S)
    x = x.reshape(N, D_MODEL)
    for i in range(DEPTH):
        kind, j = i % N_MIXERS, i // N_MIXERS
        if kind == 0:
            wd, w1, w2, wk, wvt, tabs = _mla_prep(w['a_w_dq'][j], w['a_w_dkv'][j], w['a_w_uq'][j], w['a_w_ukv'][j], S)
            cq, ckv, kr = _mla_down(x, wd, w['a_g_q'][j][None, :], w['a_g_kv'][j][None, :], tabs['k_cos'], tabs['k_sin'], S, bm)
            q = _mla_q_up(cq, w1, w2, tabs['q_cos'], tabs['q_sin'], B, S, bm)
            k, vt = _mla_kv_up(ckv, kr, wk, wvt, B, S, bm)
            a = _flash(q, k, vt, hp=2, G=1, tq=tq)
            w_o = w['a_w_o'][j]
        elif kind == 1:
            prep = _gqa_prep(w['b_w_qkv'][j], w['b_g_q'][j], w['b_g_k'][j], S)
            q, k, vt = _gqa_proj(x, *prep, B, S, bm)
            a = _flash(q, k, vt, hp=1, G=B_Q_HEADS // B_KV_HEADS, tq=min(256, S))
            w_o = w['b_w_o'][j]
        else:
            q, k, v = _nbr_proj(x, w['c_w_qkv'][j].astype(BF16), B, S, bm)
            a = _nbr_attention(q, k, v, _nbr_bias(w['c_rpb'][j]))
            w_o = w['c_w_o'][j]
        x = _out_ln(a.reshape(N, D_MODEL), w_o.astype(BF16), x, w['ln1_g'][i][None, :], w['ln1_b'][i][None, :], bm)
        f_i = i // 2
        pi = p[i].reshape(N, PLE_DIM)
        wpg, wpi = w['ple_w_gate'][i].astype(BF16), w['ple_w_in'][i].astype(BF16)
        g2, b2 = w['ln2_g'][i][None, :], w['ln2_b'][i][None, :]
        if i % 2 == 0:
            h = _swiglu_up(x, w['f_w_gate'][f_i].astype(BF16), w['f_w_up'][f_i].astype(BF16), bm, FF_DENSE // 2)
            x = _ffn_down_ln(h, w['f_w_down'][f_i].astype(BF16), x, pi, wpg, wpi, g2, b2, bm)
        else:
            w_r = jnp.pad(w['m_w_router'][f_i], ((0, 0), (0, 128 - N_EXPERTS)))
            idx, gates = _router(x, w_r, bm)
            slot, row_token, tile_expert, n_tiles = _route(idx[:, :TOP_K], MOE_TILE)
            xs = jnp.take(x.astype(BF16), row_token, axis=0)
            ys = _moe_ffn(tile_expert, n_tiles, xs, w['m_w_gate'][f_i].astype(BF16), w['m_w_up'][f_i].astype(BF16),
                          w['m_w_down'][f_i].astype(BF16), MOE_TILE)
            ya = jnp.take(ys, slot[:, 0], axis=0)
            yb = jnp.take(ys, slot[:, 1], axis=0)
            x = _moe_combine_ln(ya, yb, gates, x, pi, wpg, wpi, g2, b2, bm)
    return x.reshape(B, S, D_MODEL)


def kernel(x_prompt, x_sample, p_prompt, p_sample, a_w_dq, a_g_q, a_w_uq, a_w_dkv, a_g_kv, a_w_ukv, a_w_o, b_w_qkv, b_g_q, b_g_k, b_w_o, c_w_qkv, c_rpb, c_w_o, ln1_g, ln1_b, ln2_g, ln2_b, f_w_gate, f_w_up, f_w_down, m_w_router, m_w_gate, m_w_up, m_w_down, ple_w_gate, ple_w_in):
    w = dict(a_w_dq=a_w_dq, a_g_q=a_g_q, a_w_uq=a_w_uq, a_w_dkv=a_w_dkv, a_g_kv=a_g_kv, a_w_ukv=a_w_ukv, a_w_o=a_w_o,
             b_w_qkv=b_w_qkv, b_g_q=b_g_q, b_g_k=b_g_k, b_w_o=b_w_o, c_w_qkv=c_w_qkv, c_rpb=c_rpb, c_w_o=c_w_o,
             ln1_g=ln1_g, ln1_b=ln1_b, ln2_g=ln2_g, ln2_b=ln2_b, f_w_gate=f_w_gate, f_w_up=f_w_up, f_w_down=f_w_down,
             m_w_router=m_w_router, m_w_gate=m_w_gate, m_w_up=m_w_up, m_w_down=m_w_down,
             ple_w_gate=ple_w_gate, ple_w_in=ple_w_in)
    return (_trunk(x_prompt, p_prompt, w), _trunk(x_sample, p_sample, w))
```

```python
import functools

import jax
import jax.numpy as jnp
from jax import lax
from jax.experimental import pallas as pl
from jax.experimental.pallas import tpu as pltpu

F32 = jnp.float32
BF16 = jnp.bfloat16

D_MODEL = 1024
DEPTH = 4
GRID_W = 64
PLE_DIM = 256
N_MIXERS = 3
ROPE_THETA = 10000.0
RMS_EPS = 1e-6
LN_EPS = 1e-5
DEEPNORM_ALPHA = (2 * DEPTH) ** 0.25

A_HEADS = 16
A_Q_LORA = 384
A_KV_LORA = 256
A_NOPE = 64
A_ROPE = 32
A_V = 64
A_DK = 128
B_Q_HEADS = 16
B_KV_HEADS = 4
B_HEAD_DIM = 64
C_HEADS = 16
C_HEAD_DIM = 64
C_WIN_H = 8
C_WIN_W = 16
FF_DENSE = 2816
N_EXPERTS = 8
TOP_K = 2
FF_EXPERT = 3584

V_ROWS = 80
NEG = -1e30
LOG2E = 1.4426950408889634
VMEM_LIMIT = 56 * 1024 * 1024
ROW_BLOCK = 512
KV_CHUNK = 512
MOE_TILE = 512
NBR_Q = 2 * GRID_W
NBR_KBLK = (C_WIN_H + 2) // 2
NT = (((1,), (1,)), ((), ()))


def _params(*sem):
    return pltpu.CompilerParams(dimension_semantics=sem, vmem_limit_bytes=VMEM_LIMIT)


def _full(shape):
    return pl.BlockSpec(shape, lambda *_: (0,) * len(shape))


def _rms(x, g):
    return x * lax.rsqrt(jnp.mean(x * x, axis=-1, keepdims=True) + RMS_EPS) * g


def _layer_norm(x, g, b):
    mu = jnp.mean(x, axis=-1, keepdims=True)
    xc = x - mu
    var = jnp.mean(xc * xc, axis=-1, keepdims=True)
    return xc * lax.rsqrt(var + LN_EPS) * g + b


def _tile_lanes(t, n):
    return jnp.concatenate([t] * n, axis=-1)


def _mla_down_kernel(x_ref, w_ref, gq_ref, gkv_ref, cos_ref, sin_ref, cq_ref, ckv_ref, kr_ref):
    acc = jnp.dot(x_ref[...].astype(BF16), w_ref[...], preferred_element_type=F32)
    cq_ref[...] = _rms(acc[:, :A_Q_LORA], gq_ref[...]).astype(BF16)
    ckv_ref[...] = _rms(acc[:, A_Q_LORA:640], gkv_ref[...]).astype(BF16)
    kr_ref[...] = (acc[:, 640:768] * cos_ref[...] + acc[:, 768:896] * sin_ref[...]).astype(BF16)


def _mla_down(x, w, gq, gkv, cos_t, sin_t, S, bm):
    N = x.shape[0]
    nsb = S // bm
    row = lambda i: (i, 0)
    tab = lambda i: (i % nsb, 0)
    return pl.pallas_call(
        _mla_down_kernel,
        grid=(N // bm,),
        in_specs=[pl.BlockSpec((bm, D_MODEL), row), _full(w.shape), _full(gq.shape), _full(gkv.shape),
                  pl.BlockSpec((bm, 128), tab), pl.BlockSpec((bm, 128), tab)],
        out_specs=[pl.BlockSpec((bm, A_Q_LORA), row), pl.BlockSpec((bm, A_KV_LORA), row),
                   pl.BlockSpec((bm, 128), row)],
        out_shape=[jax.ShapeDtypeStruct((N, A_Q_LORA), BF16), jax.ShapeDtypeStruct((N, A_KV_LORA), BF16),
                   jax.ShapeDtypeStruct((N, 128), BF16)],
        compiler_params=_params("parallel"),
        name="mla_down",
    )(x, w, gq, gkv, cos_t, sin_t)


def _mla_q_up_kernel(cq_ref, w1_ref, w2_ref, cos_ref, sin_ref, q_ref):
    cq = cq_ref[...]
    a1 = jnp.dot(cq, w1_ref[...], preferred_element_type=F32)
    a2 = jnp.dot(cq, w2_ref[...], preferred_element_type=F32)
    q = a1 * _tile_lanes(cos_ref[...], A_HEADS) + a2 * _tile_lanes(sin_ref[...], A_HEADS)
    for h in range(A_HEADS):
        q_ref[0, h] = q[:, h * A_DK:(h + 1) * A_DK].astype(BF16)


def _mla_q_up(cq, w1, w2, cos_t, sin_t, B, S, bm):
    nsb = S // bm
    return pl.pallas_call(
        _mla_q_up_kernel,
        grid=(B * nsb,),
        in_specs=[pl.BlockSpec((bm, A_Q_LORA), lambda i: (i, 0)), _full(w1.shape), _full(w2.shape),
                  pl.BlockSpec((bm, 128), lambda i: (i % nsb, 0)), pl.BlockSpec((bm, 128), lambda i: (i % nsb, 0))],
        out_specs=pl.BlockSpec((1, A_HEADS, bm, A_DK), lambda i: (i // nsb, 0, i % nsb, 0)),
        out_shape=jax.ShapeDtypeStruct((B, A_HEADS, S, A_DK), BF16),
        compiler_params=_params("parallel"),
        name="mla_q_up",
    )(cq, w1, w2, cos_t, sin_t)


def _ones_rows(n):
    r = lax.broadcasted_iota(jnp.int32, (V_ROWS - 64, n), 0)
    return jnp.where(r == 0, 1.0, 0.0).astype(BF16)


def _mla_kv_up_kernel(ckv_ref, kr_ref, wk_ref, wvt_ref, k_ref, vt_ref):
    ckv = ckv_ref[...]
    ak = jnp.dot(ckv, wk_ref[...], preferred_element_type=F32)
    kr = kr_ref[...].astype(F32)
    avt = lax.dot_general(wvt_ref[...], ckv, NT, preferred_element_type=F32)
    ones = _ones_rows(ckv.shape[0])
    for h in range(A_HEADS):
        k_ref[0, h] = (ak[:, h * A_DK:(h + 1) * A_DK] + kr).astype(BF16)
        vt_ref[0, h, 0, 0:A_V, :] = avt[h * A_V:(h + 1) * A_V, :].astype(BF16)
        vt_ref[0, h, 0, A_V:V_ROWS, :] = ones


def _mla_kv_up(ckv, kr, wk, wvt, B, S, bm):
    nsb = S // bm
    return pl.pallas_call(
        _mla_kv_up_kernel,
        grid=(B * nsb,),
        in_specs=[pl.BlockSpec((bm, A_KV_LORA), lambda i: (i, 0)), pl.BlockSpec((bm, 128), lambda i: (i, 0)),
                  _full(wk.shape), _full(wvt.shape)],
        out_specs=[pl.BlockSpec((1, A_HEADS, bm, A_DK), lambda i: (i // nsb, 0, i % nsb, 0)),
                   pl.BlockSpec((1, A_HEADS, 1, V_ROWS, bm), lambda i: (i // nsb, 0, i % nsb, 0, 0))],
        out_shape=[jax.ShapeDtypeStruct((B, A_HEADS, S, A_DK), BF16),
                   jax.ShapeDtypeStruct((B, A_HEADS, nsb, V_ROWS, bm), BF16)],
        compiler_params=_params("parallel"),
        name="mla_kv_up",
    )(ckv, kr, wk, wvt)


def _gqa_proj_kernel(x_ref, wq_ref, wqp_ref, wk_ref, wvt_ref, ones_ref, gq_ref, gqp_ref, gk_ref, gkp_ref,
                     cos_ref, sin_ref, q_ref, k_ref, vt_ref):
    xb = x_ref[...].astype(BF16)
    nq = B_Q_HEADS * B_HEAD_DIM
    nk = B_KV_HEADS * B_HEAD_DIM
    cos = cos_ref[...]
    sin = sin_ref[...]
    ones_bd = ones_ref[...]

    def norm_rope(a, ap, g, gp, n):
        ss = jnp.dot((a * a).astype(BF16), ones_bd[:n, :n], preferred_element_type=F32) * (1.0 / B_HEAD_DIM)
        r = lax.rsqrt(ss + RMS_EPS)
        reps = n // 128
        return r * (a * g * _tile_lanes(cos, reps) + ap * gp * _tile_lanes(sin, reps))

    aq = jnp.dot(xb, wq_ref[...], preferred_element_type=F32)
    aqp = jnp.dot(xb, wqp_ref[...], preferred_element_type=F32)
    q = norm_rope(aq, aqp, gq_ref[...], gqp_ref[...], nq) * (B_HEAD_DIM ** -0.5 * LOG2E)
    akk = jnp.dot(xb, wk_ref[...], preferred_element_type=F32)
    k = norm_rope(akk[:, :nk], akk[:, nk:], gk_ref[...], gkp_ref[...], nk)
    avt = lax.dot_general(wvt_ref[...], xb, NT, preferred_element_type=F32)
    ones = _ones_rows(xb.shape[0])
    for h in range(B_Q_HEADS):
        q_ref[0, h] = q[:, h * B_HEAD_DIM:(h + 1) * B_HEAD_DIM].astype(BF16)
    for h in range(B_KV_HEADS):
        k_ref[0, h] = k[:, h * B_HEAD_DIM:(h + 1) * B_HEAD_DIM].astype(BF16)
        vt_ref[0, h, 0, 0:B_HEAD_DIM, :] = avt[h * B_HEAD_DIM:(h + 1) * B_HEAD_DIM, :].astype(BF16)
        vt_ref[0, h, 0, B_HEAD_DIM:V_ROWS, :] = ones


def _gqa_proj(x, wq, wqp, wk, wvt, ones_bd, gq, gqp, gk, gkp, cos_t, sin_t, B, S, bm):
    nsb = S // bm
    hm = lambda i: (i // nsb, 0, i % nsb, 0)
    tab = lambda i: (i % nsb, 0)
    ins = [x, wq, wqp, wk, wvt, ones_bd, gq, gqp, gk, gkp, cos_t, sin_t]
    specs = [pl.BlockSpec((bm, D_MODEL), lambda i: (i, 0))] + [_full(a.shape) for a in ins[1:10]]
    specs += [pl.BlockSpec((bm, 128), tab), pl.BlockSpec((bm, 128), tab)]
    return pl.pallas_call(
        _gqa_proj_kernel,
        grid=(B * nsb,),
        in_specs=specs,
        out_specs=[pl.BlockSpec((1, B_Q_HEADS, bm, B_HEAD_DIM), hm), pl.BlockSpec((1, B_KV_HEADS, bm, B_HEAD_DIM), hm),
                   pl.BlockSpec((1, B_KV_HEADS, 1, V_ROWS, bm), lambda i: (i // nsb, 0, i % nsb, 0, 0))],
        out_shape=[jax.ShapeDtypeStruct((B, B_Q_HEADS, S, B_HEAD_DIM), BF16),
                   jax.ShapeDtypeStruct((B, B_KV_HEADS, S, B_HEAD_DIM), BF16),
                   jax.ShapeDtypeStruct((B, B_KV_HEADS, nsb, V_ROWS, bm), BF16)],
        compiler_params=_params("parallel"),
        name="gqa_proj",
    )(*ins)


def _flash_kernel(q_ref, k_ref, vt_ref, o_ref, s_a, s_b, p_a, p_b, acc_ref, *, hp, G, tq, tk, nk, dv, unroll):
    TQ = G * tq
    dk = q_ref.shape[-1]
    outs = []
    for h in range(hp):
        q = q_ref[0, h * G:(h + 1) * G].reshape(TQ, dk)

        def scores(c, s_ref, h=h, q=q):
            off = pl.multiple_of(jnp.minimum(c, nk - 1) * tk, tk)
            s = lax.dot_general(k_ref[0, h, pl.ds(off, tk), :], q, NT, preferred_element_type=F32)
            s_ref[...] = s
            return jnp.max(s, axis=0, keepdims=True)

        def step(c, m, a_prev, s_cur, p_cur, s_nxt, p_prv, h=h):
            pv = jnp.dot(vt_ref[0, h, jnp.maximum(c - 1, 0)], p_prv[...], preferred_element_type=F32)
            acc_ref[...] = a_prev * acc_ref[...] + pv
            cm = scores(c + 1, s_nxt)
            p_cur[...] = jnp.exp2((s_cur[...] - m).astype(BF16))
            m_new = jnp.maximum(m, cm)
            return m_new, jnp.exp2(m - m_new)

        def group(jj, carry):
            m, a_prev, a_cur = carry
            a = [a_prev, a_cur]
            for i in range(unroll):
                even = (s_a, p_a, s_b, p_b)
                odd = (s_b, p_b, s_a, p_a)
                m, a_new = step(unroll * jj + i, m, a[i], *(even if i % 2 == 0 else odd))
                a.append(a_new)
            return m, a[unroll], a[unroll + 1]

        acc_ref[...] = jnp.zeros_like(acc_ref)
        p_b[...] = jnp.zeros_like(p_b)
        m0 = jnp.maximum(scores(0, s_a), NEG)
        one = jnp.ones((1, TQ), F32)
        _, a_prev, _ = lax.fori_loop(0, nk // unroll, group, (m0, one, one))
        acc = a_prev * acc_ref[...] + jnp.dot(vt_ref[0, h, nk - 1], p_b[...], preferred_element_type=F32)
        o = acc[:dv] * (1.0 / acc[dv:dv + 1])
        for g in range(G):
            outs.append(o[:, g * tq:(g + 1) * tq])
    o_ref[0] = jnp.concatenate(outs, axis=0).T.astype(o_ref.dtype)


def _flash(q, k, vt, *, hp, G, tq, dv=64):
    B, H, S, dk = q.shape
    Hk = k.shape[1]
    nk, tk = vt.shape[2], vt.shape[4]
    unroll = 4 if nk % 4 == 0 else 2
    assert nk % unroll == 0
    TQ = G * tq
    scratch = [pltpu.VMEM((tk, TQ), F32), pltpu.VMEM((tk, TQ), F32), pltpu.VMEM((tk, TQ), BF16),
               pltpu.VMEM((tk, TQ), BF16), pltpu.VMEM((V_ROWS, TQ), F32)]
    kern = functools.partial(_flash_kernel, hp=hp, G=G, tq=tq, tk=tk, nk=nk, dv=dv, unroll=unroll)
    return pl.pallas_call(
        kern,
        grid=(B, Hk // hp, S // tq),
        in_specs=[pl.BlockSpec((1, hp * G, tq, dk), lambda b, h, i: (b, h, i, 0)),
                  pl.BlockSpec((1, hp, S, dk), lambda b, h, i: (b, h, 0, 0)),
                  pl.BlockSpec((1, hp, nk, V_ROWS, tk), lambda b, h, i: (b, h, 0, 0, 0))],
        out_specs=pl.BlockSpec((1, tq, hp * G * dv), lambda b, h, i: (b, i, h)),
        out_shape=jax.ShapeDtypeStruct((B, S, H * dv), BF16),
        scratch_shapes=scratch,
        compiler_params=_params("parallel", "parallel", "arbitrary"),
        name="flash_attention",
    )(q, k, vt)


def _nbr_proj_kernel(x_ref, w_ref, wvt_ref, q_ref, k_ref, vt_ref):
    xb = x_ref[...].astype(BF16)
    n = C_HEADS * C_HEAD_DIM
    acc = jnp.dot(xb, w_ref[...], preferred_element_type=F32)
    avt = lax.dot_general(wvt_ref[...], xb, NT, preferred_element_type=F32)
    ones = _ones_rows(NBR_Q)
    for h in range(C_HEADS):
        lo = h * C_HEAD_DIM
        q_ref[0, h] = (acc[:, lo:lo + C_HEAD_DIM] * (C_HEAD_DIM ** -0.5)).astype(BF16)
        k_ref[0, h] = acc[:, n + lo:n + lo + C_HEAD_DIM].astype(BF16)
        for c in range(xb.shape[0] // NBR_Q):
            vt_ref[0, h, c, 0:C_HEAD_DIM, :] = avt[lo:lo + C_HEAD_DIM, c * NBR_Q:(c + 1) * NBR_Q].astype(BF16)
            vt_ref[0, h, c, C_HEAD_DIM:V_ROWS, :] = ones


def _nbr_proj(x, w, wvt, B, S, bm):
    nsb = S // bm
    hm = lambda i: (i // nsb, 0, i % nsb, 0)
    shp = jax.ShapeDtypeStruct((B, C_HEADS, S, C_HEAD_DIM), BF16)
    return pl.pallas_call(
        _nbr_proj_kernel,
        grid=(B * nsb,),
        in_specs=[pl.BlockSpec((bm, D_MODEL), lambda i: (i, 0)), _full(w.shape), _full(wvt.shape)],
        out_specs=[pl.BlockSpec((1, C_HEADS, bm, C_HEAD_DIM), hm)] * 2
        + [pl.BlockSpec((1, C_HEADS, bm // NBR_Q, V_ROWS, NBR_Q), lambda i: (i // nsb, 0, i % nsb, 0, 0))],
        out_shape=[shp, shp, jax.ShapeDtypeStruct((B, C_HEADS, S // NBR_Q, V_ROWS, NBR_Q), BF16)],
        compiler_params=_params("parallel"),
        name="nbr_proj",
    )(x, w, wvt)


def _nbr_kernel(var_ref, q_ref, *refs):
    del var_ref
    k_refs = refs[:NBR_KBLK]
    vt_refs = refs[NBR_KBLK:2 * NBR_KBLK]
    bias_ref, o_ref = refs[2 * NBR_KBLK], refs[2 * NBR_KBLK + 1]
    k = jnp.concatenate([r[0] for r in k_refs], axis=1)
    vt = jnp.concatenate([r[0, :, 0] for r in vt_refs], axis=2)
    s = jnp.einsum('hkd,hqd->hkq', k, q_ref[0], preferred_element_type=F32) + bias_ref[0]
    m = jnp.max(s, axis=1, keepdims=True)
    p = jnp.exp(s - m).astype(BF16)
    acc = jnp.einsum('hdk,hkq->hdq', vt, p, preferred_element_type=F32)
    o = acc[:, :C_HEAD_DIM] * (1.0 / acc[:, C_HEAD_DIM:C_HEAD_DIM + 1])
    o_ref[0] = o.reshape(C_HEADS * C_HEAD_DIM, NBR_Q).T.astype(o_ref.dtype)


def _nbr_windows(rows):
    nkr = 2 * NBR_KBLK
    sigs, var_of_u = [], []
    for u in range(rows // 2):
        ks = min(max(2 * u - C_WIN_H // 2, 0), rows - nkr)
        sig = []
        for a in range(2):
            r = 2 * u + a
            rs = min(max(r - C_WIN_H // 2, 0), rows - C_WIN_H)
            sig.append((ks - r, ks - rs))
        sig = tuple(sig)
        if sig not in sigs:
            sigs.append(sig)
        var_of_u.append(sigs.index(sig))
    row_idx = [[[min(max(i + dr + C_WIN_H - 1, 0), 2 * C_WIN_H - 2) for i in range(nkr)] for dr, _ in sig] for sig in sigs]
    inside = [[[0 <= i + ds < C_WIN_H for i in range(nkr)] for _, ds in sig] for sig in sigs]
    return var_of_u, row_idx, inside


def _nbr_bias(rpb, rows):
    kw, W = C_WIN_W, GRID_W
    var_of_u, row_idx, inside_row = _nbr_windows(rows)
    qc = jnp.arange(W)[:, None]
    kc = jnp.arange(W)[None, :]
    c_start = jnp.clip(qc - kw // 2, 0, W - kw)
    inside_col = (kc >= c_start) & (kc < c_start + kw)
    col_idx = jnp.clip(kc - qc + (kw - 1), 0, 2 * kw - 2)
    row_sel = jax.nn.one_hot(jnp.asarray(row_idx, jnp.int32), 2 * C_WIN_H - 1, dtype=F32)
    col_sel = jax.nn.one_hot(col_idx, 2 * kw - 1, dtype=F32)
    b = jnp.einsum('hrc,vair->hvaic', rpb, row_sel, precision=lax.Precision.HIGHEST)
    b = jnp.einsum('hvaic,qkc->hvaiqk', b, col_sel, precision=lax.Precision.HIGHEST)
    ok = jnp.asarray(inside_row)[None, :, :, :, None, None] & inside_col[None, None, None, None]
    b = jnp.where(ok, b, NEG).transpose(1, 0, 3, 5, 2, 4)
    nv = len(row_idx)
    return b.reshape(nv, C_HEADS, 2 * NBR_KBLK * W, NBR_Q).astype(F32), jnp.asarray(var_of_u, jnp.int32)


def _nbr_attention(q, k, vt, bias, var_of_u):
    B, H, S, d = q.shape
    nu = S // NBR_Q

    def first(u):
        return jnp.clip(u - C_WIN_H // 4, 0, nu - NBR_KBLK)

    k_specs = [pl.BlockSpec((1, H, NBR_Q, d), functools.partial(lambda b, u, var, i: (b, 0, first(u) + i, 0), i=i))
               for i in range(NBR_KBLK)]
    vt_specs = [pl.BlockSpec((1, H, 1, V_ROWS, NBR_Q), functools.partial(lambda b, u, var, i: (b, 0, first(u) + i, 0, 0), i=i))
                for i in range(NBR_KBLK)]
    grid_spec = pltpu.PrefetchScalarGridSpec(
        num_scalar_prefetch=1,
        grid=(B, nu),
        in_specs=[pl.BlockSpec((1, H, NBR_Q, d), lambda b, u, var: (b, 0, u, 0))] + k_specs + vt_specs
        + [pl.BlockSpec((1, H, NBR_KBLK * NBR_Q, NBR_Q), lambda b, u, var: (var[u], 0, 0, 0))],
        out_specs=pl.BlockSpec((1, NBR_Q, H * d), lambda b, u, var: (b, u, 0)),
    )
    return pl.pallas_call(
        _nbr_kernel,
        grid_spec=grid_spec,
        out_shape=jax.ShapeDtypeStruct((B, S, H * d), BF16),
        compiler_params=_params("parallel", "arbitrary"),
        name="nbr_attention",
    )(var_of_u, q, *([k] * NBR_KBLK), *([vt] * NBR_KBLK), bias)


def _out_ln_kernel(a_ref, w_ref, x_ref, g_ref, b_ref, o_ref):
    h = jnp.dot(a_ref[...], w_ref[...], preferred_element_type=F32)
    o_ref[...] = _layer_norm(DEEPNORM_ALPHA * x_ref[...] + h, g_ref[...], b_ref[...])


def _out_ln(a, w, x, g, b, bm):
    N = x.shape[0]
    row = lambda i: (i, 0)
    return pl.pallas_call(
        _out_ln_kernel,
        grid=(N // bm,),
        in_specs=[pl.BlockSpec((bm, a.shape[1]), row), _full(w.shape), pl.BlockSpec((bm, D_MODEL), row),
                  _full(g.shape), _full(b.shape)],
        out_specs=pl.BlockSpec((bm, D_MODEL), row),
        out_shape=jax.ShapeDtypeStruct((N, D_MODEL), F32),
        compiler_params=_params("parallel"),
        name="out_proj_ln",
    )(a, w, x, g, b)


def _swiglu_up_kernel(x_ref, wg_ref, wu_ref, h_ref):
    xb = x_ref[...].astype(BF16)
    g = jnp.dot(xb, wg_ref[...], preferred_element_type=F32)
    u = jnp.dot(xb, wu_ref[...], preferred_element_type=F32)
    h_ref[...] = (g * jax.nn.sigmoid(g) * u).astype(BF16)


def _swiglu_up(x, wg, wu, bm, bn):
    N = x.shape[0]
    F = wg.shape[1]
    return pl.pallas_call(
        _swiglu_up_kernel,
        grid=(F // bn, N // bm),
        in_specs=[pl.BlockSpec((bm, D_MODEL), lambda j, i: (i, 0)), pl.BlockSpec((D_MODEL, bn), lambda j, i: (0, j)),
                  pl.BlockSpec((D_MODEL, bn), lambda j, i: (0, j))],
        out_specs=pl.BlockSpec((bm, bn), lambda j, i: (i, j)),
        out_shape=jax.ShapeDtypeStruct((N, F), BF16),
        compiler_params=_params("parallel", "parallel"),
        name="swiglu_up",
    )(x, wg, wu)


def _ple(x, p_ref, wpg_ref, wpi_ref):
    gate = jax.nn.sigmoid(jnp.dot(x.astype(BF16), wpg_ref[...], preferred_element_type=F32))
    return gate * jnp.dot(p_ref[...].astype(BF16), wpi_ref[...], preferred_element_type=F32)


def _ffn_down_ln_kernel(h_ref, wd_ref, x_ref, p_ref, wpg_ref, wpi_ref, g_ref, b_ref, o_ref):
    x = x_ref[...]
    f = jnp.dot(h_ref[...], wd_ref[...], preferred_element_type=F32)
    o_ref[...] = _layer_norm(DEEPNORM_ALPHA * x + f + _ple(x, p_ref, wpg_ref, wpi_ref), g_ref[...], b_ref[...])


def _ffn_down_ln(h, wd, x, p, wpg, wpi, g, b, bm):
    N = x.shape[0]
    row = lambda i: (i, 0)
    return pl.pallas_call(
        _ffn_down_ln_kernel,
        grid=(N // bm,),
        in_specs=[pl.BlockSpec((bm, h.shape[1]), row), _full(wd.shape), pl.BlockSpec((bm, D_MODEL), row),
                  pl.BlockSpec((bm, PLE_DIM), row), _full(wpg.shape), _full(wpi.shape), _full(g.shape), _full(b.shape)],
        out_specs=pl.BlockSpec((bm, D_MODEL), row),
        out_shape=jax.ShapeDtypeStruct((N, D_MODEL), F32),
        compiler_params=_params("parallel"),
        name="ffn_down_ple_ln",
    )(h, wd, x, p, wpg, wpi, g, b)


def _router_kernel(x_ref, w_ref, idx_ref, gate_ref):
    logits = jnp.dot(x_ref[...], w_ref[...], preferred_element_type=F32, precision=lax.Precision.HIGHEST)
    lane = lax.broadcasted_iota(jnp.int32, logits.shape, 1).astype(F32)
    logits = jnp.where(lane < N_EXPERTS, logits, NEG)
    m1 = jnp.max(logits, axis=-1, keepdims=True)
    i1 = jnp.min(jnp.where(logits == m1, lane, 128.0), axis=-1, keepdims=True)
    rest = jnp.where(lane == i1, NEG, logits)
    m2 = jnp.max(rest, axis=-1, keepdims=True)
    i2 = jnp.min(jnp.where(rest == m2, lane, 128.0), axis=-1, keepdims=True)
    e = jnp.exp(m2 - m1)
    g1 = 1.0 / (1.0 + e)
    idx_ref[...] = jnp.where(lane == 0.0, i1, jnp.where(lane == 1.0, i2, 0.0)).astype(jnp.int32)
    gate_ref[...] = jnp.where(lane == 0.0, g1, jnp.where(lane == 1.0, e * g1, 0.0))


def _router(x, w, bm):
    N = x.shape[0]
    row = lambda i: (i, 0)
    return pl.pallas_call(
        _router_kernel,
        grid=(N // bm,),
        in_specs=[pl.BlockSpec((bm, D_MODEL), row), _full(w.shape)],
        out_specs=[pl.BlockSpec((bm, 128), row)] * 2,
        out_shape=[jax.ShapeDtypeStruct((N, 128), jnp.int32), jax.ShapeDtypeStruct((N, 128), F32)],
        compiler_params=_params("parallel"),
        name="moe_router",
    )(x, w)


def _moe_ffn_kernel(te_ref, nt_ref, xs_ref, wg_ref, wu_ref, wd_ref, o_ref, acc_ref):
    t = pl.program_id(0)
    j = pl.program_id(1)

    @pl.when(t < nt_ref[0])
    def _():
        xs = xs_ref[...]
        g = jnp.dot(xs, wg_ref[0], preferred_element_type=F32)
        u = jnp.dot(xs, wu_ref[0], preferred_element_type=F32)
        h = (g * jax.nn.sigmoid(g) * u).astype(BF16)
        y = jnp.dot(h, wd_ref[0], preferred_element_type=F32)

        @pl.when(j == 0)
        def _():
            acc_ref[...] = y

        @pl.when(j == pl.num_programs(1) - 1)
        def _():
            o_ref[...] = acc_ref[...] + y

    @pl.when(t >= nt_ref[0])
    def _():
        o_ref[...] = jnp.zeros_like(o_ref)


def _moe_ffn(tile_expert, n_tiles, xs, wg, wu, wd, tm, halves=2):
    P = xs.shape[0]
    fh = FF_EXPERT // halves
    grid_spec = pltpu.PrefetchScalarGridSpec(
        num_scalar_prefetch=2,
        grid=(P // tm, halves),
        in_specs=[pl.BlockSpec((tm, D_MODEL), lambda t, j, te, nt: (t, 0)),
                  pl.BlockSpec((1, D_MODEL, fh), lambda t, j, te, nt: (te[t], 0, j)),
                  pl.BlockSpec((1, D_MODEL, fh), lambda t, j, te, nt: (te[t], 0, j)),
                  pl.BlockSpec((1, fh, D_MODEL), lambda t, j, te, nt: (te[t], j, 0))],
        out_specs=pl.BlockSpec((tm, D_MODEL), lambda t, j, te, nt: (t, 0)),
        scratch_shapes=[pltpu.VMEM((tm, D_MODEL), F32)],
    )
    return pl.pallas_call(
        _moe_ffn_kernel,
        grid_spec=grid_spec,
        out_shape=jax.ShapeDtypeStruct((P, D_MODEL), F32),
        compiler_params=_params("arbitrary", "arbitrary"),
        name="moe_expert_ffn",
    )(tile_expert, n_tiles, xs, wg, wu, wd)


def _moe_combine_ln_kernel(ya_ref, yb_ref, gate_ref, x_ref, p_ref, wpg_ref, wpi_ref, g_ref, b_ref, o_ref):
    x = x_ref[...]
    gate = gate_ref[...]
    f = gate[:, 0:1] * ya_ref[...] + gate[:, 1:2] * yb_ref[...]
    o_ref[...] = _layer_norm(DEEPNORM_ALPHA * x + f + _ple(x, p_ref, wpg_ref, wpi_ref), g_ref[...], b_ref[...])


def _moe_combine_ln(ya, yb, gates, x, p, wpg, wpi, g, b, bm):
    N = x.shape[0]
    row = lambda i: (i, 0)
    rows = pl.BlockSpec((bm, D_MODEL), row)
    return pl.pallas_call(
        _moe_combine_ln_kernel,
        grid=(N // bm,),
        in_specs=[rows, rows, pl.BlockSpec((bm, 128), row), rows, pl.BlockSpec((bm, PLE_DIM), row),
                  _full(wpg.shape), _full(wpi.shape), _full(g.shape), _full(b.shape)],
        out_specs=rows,
        out_shape=jax.ShapeDtypeStruct((N, D_MODEL), F32),
        compiler_params=_params("parallel"),
        name="moe_combine_ple_ln",
    )(ya, yb, gates, x, p, wpg, wpi, g, b)


def _route(idx, tm):
    N = idx.shape[0]
    e_flat = idx.reshape(-1)
    onehot = (e_flat[:, None] == jnp.arange(N_EXPERTS, dtype=jnp.int32)[None, :]).astype(jnp.int32)
    csum = jnp.cumsum(onehot, axis=0)
    rank = jnp.sum(csum * onehot, axis=1) - 1
    counts = csum[-1]
    padded = ((counts + tm - 1) // tm) * tm
    ends = jnp.cumsum(padded)
    slot = (ends - padded)[e_flat] + rank
    P = TOP_K * N + N_EXPERTS * tm
    row_token = jnp.zeros((P,), jnp.int32).at[slot].set(jnp.arange(TOP_K * N, dtype=jnp.int32) // TOP_K)
    tile_start = jnp.arange(P // tm, dtype=jnp.int32) * tm
    tile_expert = jnp.minimum(jnp.sum((tile_start[:, None] >= ends[None, :]).astype(jnp.int32), axis=1), N_EXPERTS - 1)
    n_tiles = (ends[-1] // tm).astype(jnp.int32).reshape(1)
    return slot.reshape(N, TOP_K), row_token, tile_expert.astype(jnp.int32), n_tiles


def _swap_halves(w, group):
    shp = w.shape
    w = w.reshape(shp[:-1] + (shp[-1] // group, 2, group // 2))
    return jnp.flip(w, axis=-2).reshape(shp)


def _rope_tables(pos, dim):
    inv = ROPE_THETA ** (-jnp.arange(0, dim, 2, dtype=F32) / dim)
    ang = pos.astype(F32)[:, None] * inv[None, :]
    c, s = jnp.cos(ang), jnp.sin(ang)
    return jnp.concatenate([c, c], -1), jnp.concatenate([-s, s], -1)


def _mla_prep(w_dq, w_dkv, w_uq, w_ukv, S):
    wd = jnp.zeros((D_MODEL, 896), F32)
    wd = wd.at[:, :A_Q_LORA].set(w_dq).at[:, A_Q_LORA:640].set(w_dkv[:, :A_KV_LORA])
    w_r = w_dkv[:, A_KV_LORA:]
    wd = wd.at[:, 640 + A_NOPE:640 + A_NOPE + A_ROPE].set(w_r)
    wd = wd.at[:, 768 + A_NOPE:768 + A_NOPE + A_ROPE].set(_swap_halves(w_r, A_ROPE))
    uq = w_uq.reshape(A_Q_LORA, A_HEADS, A_NOPE + A_ROPE)
    w1 = jnp.pad(uq, ((0, 0), (0, 0), (0, A_DK - A_NOPE - A_ROPE))).reshape(A_Q_LORA, A_HEADS * A_DK)
    w2 = jnp.pad(_swap_halves(uq[..., A_NOPE:], A_ROPE), ((0, 0), (0, 0), (A_NOPE, A_DK - A_NOPE - A_ROPE)))
    w2 = w2.reshape(A_Q_LORA, A_HEADS * A_DK)
    ukv = w_ukv.reshape(A_KV_LORA, A_HEADS, A_NOPE + A_V)
    wk = jnp.pad(ukv[..., :A_NOPE], ((0, 0), (0, 0), (0, A_DK - A_NOPE))).reshape(A_KV_LORA, A_HEADS * A_DK)
    wvt = ukv[..., A_NOPE:].reshape(A_KV_LORA, A_HEADS * A_V).T
    c, s = _rope_tables(jnp.arange(S), A_ROPE)
    pad = lambda t, fill: jnp.concatenate([jnp.full((S, A_NOPE), fill, F32), t, jnp.zeros((S, A_DK - A_NOPE - A_ROPE), F32)], -1)
    scale = (A_NOPE + A_ROPE) ** -0.5 * LOG2E
    tabs = dict(k_cos=pad(c, 0.0), k_sin=pad(s, 0.0), q_cos=pad(c, 1.0) * scale, q_sin=pad(s, 0.0) * scale)
    return wd.astype(BF16), w1.astype(BF16), w2.astype(BF16), wk.astype(BF16), wvt.astype(BF16), tabs


def _gqa_prep(w_qkv, g_q, g_k, S):
    nq = B_Q_HEADS * B_HEAD_DIM
    nk = B_KV_HEADS * B_HEAD_DIM
    half = B_HEAD_DIM // 2
    wq, wk, wv = w_qkv[:, :nq], w_qkv[:, nq:nq + nk], w_qkv[:, nq + nk:]
    wkk = jnp.concatenate([wk, _swap_halves(wk, half)], axis=1)
    t = jnp.arange(S)
    cr, sr = _rope_tables(t // GRID_W, half)
    cc, sc = _rope_tables(t % GRID_W, half)
    cos_t = jnp.tile(jnp.concatenate([cr, cc], -1), (1, 2))
    sin_t = jnp.tile(jnp.concatenate([sr, sc], -1), (1, 2))
    lane = jnp.arange(nq) // B_HEAD_DIM
    ones_bd = (lane[:, None] == lane[None, :]).astype(BF16)
    gq = jnp.tile(g_q, B_Q_HEADS)[None, :]
    gk = jnp.tile(g_k, B_KV_HEADS)[None, :]
    return (wq.astype(BF16), _swap_halves(wq, half).astype(BF16), wkk.astype(BF16), wv.T.astype(BF16), ones_bd,
            gq, _swap_halves(gq, half), gk, _swap_halves(gk, half), cos_t, sin_t)


def _trunk(x, p, w):
    B, S, _ = x.shape
    N = B * S
    bm = min(ROW_BLOCK, S)
    tq = min(512, S)
    x = x.reshape(N, D_MODEL)
    for i in range(DEPTH):
        kind, j = i % N_MIXERS, i // N_MIXERS
        if kind == 0:
            wd, w1, w2, wk, wvt, tabs = _mla_prep(w['a_w_dq'][j], w['a_w_dkv'][j], w['a_w_uq'][j], w['a_w_ukv'][j], S)
            cq, ckv, kr = _mla_down(x, wd, w['a_g_q'][j][None, :], w['a_g_kv'][j][None, :], tabs['k_cos'], tabs['k_sin'], S, bm)
            q = _mla_q_up(cq, w1, w2, tabs['q_cos'], tabs['q_sin'], B, S, bm)
            k, vt = _mla_kv_up(ckv, kr, wk, wvt, B, S, bm)
            a = _flash(q, k, vt, hp=2, G=1, tq=tq)
            w_o = w['a_w_o'][j]
        elif kind == 1:
            prep = _gqa_prep(w['b_w_qkv'][j], w['b_g_q'][j], w['b_g_k'][j], S)
            q, k, vt = _gqa_proj(x, *prep, B, S, bm)
            a = _flash(q, k, vt, hp=1, G=B_Q_HEADS // B_KV_HEADS, tq=min(256, S))
            w_o = w['b_w_o'][j]
        else:
            n = C_HEADS * C_HEAD_DIM
            w_qkv = w['c_w_qkv'][j]
            q, k, vt = _nbr_proj(x, w_qkv[:, :2 * n].astype(BF16), w_qkv[:, 2 * n:].T.astype(BF16), B, S, bm)
            a = _nbr_attention(q, k, vt, *_nbr_bias(w['c_rpb'][j], S // GRID_W))
            w_o = w['c_w_o'][j]
        x = _out_ln(a.reshape(N, D_MODEL), w_o.astype(BF16), x, w['ln1_g'][i][None, :], w['ln1_b'][i][None, :], bm)
        f_i = i // 2
        pi = p[i].reshape(N, PLE_DIM)
        wpg, wpi = w['ple_w_gate'][i].astype(BF16), w['ple_w_in'][i].astype(BF16)
        g2, b2 = w['ln2_g'][i][None, :], w['ln2_b'][i][None, :]
        if i % 2 == 0:
            h = _swiglu_up(x, w['f_w_gate'][f_i].astype(BF16), w['f_w_up'][f_i].astype(BF16), bm, FF_DENSE // 2)
            x = _ffn_down_ln(h, w['f_w_down'][f_i].astype(BF16), x, pi, wpg, wpi, g2, b2, bm)
        else:
            w_r = jnp.pad(w['m_w_router'][f_i], ((0, 0), (0, 128 - N_EXPERTS)))
            idx, gates = _router(x, w_r, bm)
            slot, row_token, tile_expert, n_tiles = _route(idx[:, :TOP_K], MOE_TILE)
            xs = jnp.take(x.astype(BF16), row_token, axis=0)
            ys = _moe_ffn(tile_expert, n_tiles, xs, w['m_w_gate'][f_i].astype(BF16), w['m_w_up'][f_i].astype(BF16),
                          w['m_w_down'][f_i].astype(BF16), MOE_TILE)
            ya = jnp.take(ys, slot[:, 0], axis=0)
            yb = jnp.take(ys, slot[:, 1], axis=0)
            x = _moe_combine_ln(ya, yb, gates, x, pi, wpg, wpi, g2, b2, bm)
    return x.reshape(B, S, D_MODEL)


def kernel(x_prompt, x_sample, p_prompt, p_sample, a_w_dq, a_g_q, a_w_uq, a_w_dkv, a_g_kv, a_w_ukv, a_w_o, b_w_qkv, b_g_q, b_g_k, b_w_o, c_w_qkv, c_rpb, c_w_o, ln1_g, ln1_b, ln2_g, ln2_b, f_w_gate, f_w_up, f_w_down, m_w_router, m_w_gate, m_w_up, m_w_down, ple_w_gate, ple_w_in):
    w = dict(a_w_dq=a_w_dq, a_g_q=a_g_q, a_w_uq=a_w_uq, a_w_dkv=a_w_dkv, a_g_kv=a_g_kv, a_w_ukv=a_w_ukv, a_w_o=a_w_o,
             b_w_qkv=b_w_qkv, b_g_q=b_g_q, b_g_k=b_g_k, b_w_o=b_w_o, c_w_qkv=c_w_qkv, c_rpb=c_rpb, c_w_o=c_w_o,
             ln1_g=ln1_g, ln1_b=ln1_b, ln2_g=ln2_g, ln2_b=ln2_b, f_w_gate=f_w_gate, f_w_up=f_w_up, f_w_down=f_w_down,
             m_w_router=m_w_router, m_w_gate=m_w_gate, m_w_up=m_w_up, m_w_down=m_w_down,
             ple_w_gate=ple_w_gate, ple_w_in=ple_w_in)
    return (_trunk(x_prompt, p_prompt, w), _trunk(x_sample, p_sample, w))
```

```python
import functools

import jax
import jax.numpy as jnp
from jax import lax
from jax.experimental import pallas as pl
from jax.experimental.pallas import tpu as pltpu

F32 = jnp.float32
BF16 = jnp.bfloat16

D_MODEL = 1024
DEPTH = 4
GRID_W = 64
PLE_DIM = 256
N_MIXERS = 3
ROPE_THETA = 10000.0
RMS_EPS = 1e-6
LN_EPS = 1e-5
DEEPNORM_ALPHA = (2 * DEPTH) ** 0.25

A_HEADS = 16
A_Q_LORA = 384
A_KV_LORA = 256
A_NOPE = 64
A_ROPE = 32
A_V = 64
A_DK = 128
B_Q_HEADS = 16
B_KV_HEADS = 4
B_HEAD_DIM = 64
C_HEADS = 16
C_HEAD_DIM = 64
C_WIN_H = 8
C_WIN_W = 16
FF_DENSE = 2816
N_EXPERTS = 8
TOP_K = 2
FF_EXPERT = 3584

V_ROWS = 80
NEG = -1e30
LOG2E = 1.4426950408889634
VMEM_LIMIT = 56 * 1024 * 1024
ROW_BLOCK = 512
KV_CHUNK = 512
MOE_TILE = 512
NBR_Q = 2 * GRID_W
NBR_KBLK = (C_WIN_H + 2) // 2
NT = (((1,), (1,)), ((), ()))


def _params(*sem):
    return pltpu.CompilerParams(dimension_semantics=sem, vmem_limit_bytes=VMEM_LIMIT)


def _full(shape):
    return pl.BlockSpec(shape, lambda *_: (0,) * len(shape))


def _rms(x, g):
    return x * lax.rsqrt(jnp.mean(x * x, axis=-1, keepdims=True) + RMS_EPS) * g


def _layer_norm(x, g, b):
    mu = jnp.mean(x, axis=-1, keepdims=True)
    xc = x - mu
    var = jnp.mean(xc * xc, axis=-1, keepdims=True)
    return xc * lax.rsqrt(var + LN_EPS) * g + b


def _tile_lanes(t, n):
    return jnp.concatenate([t] * n, axis=-1)


def _mla_down_kernel(x_ref, w_ref, gq_ref, gkv_ref, cos_ref, sin_ref, cq_ref, ckv_ref, kr_ref):
    acc = jnp.dot(x_ref[...].astype(BF16), w_ref[...], preferred_element_type=F32)
    cq_ref[...] = _rms(acc[:, :A_Q_LORA], gq_ref[...]).astype(BF16)
    ckv_ref[...] = _rms(acc[:, A_Q_LORA:640], gkv_ref[...]).astype(BF16)
    kr_ref[...] = (acc[:, 640:768] * cos_ref[...] + acc[:, 768:896] * sin_ref[...]).astype(BF16)


def _mla_down(x, w, gq, gkv, cos_t, sin_t, S, bm):
    N = x.shape[0]
    nsb = S // bm
    row = lambda i: (i, 0)
    tab = lambda i: (i % nsb, 0)
    return pl.pallas_call(
        _mla_down_kernel,
        grid=(N // bm,),
        in_specs=[pl.BlockSpec((bm, D_MODEL), row), _full(w.shape), _full(gq.shape), _full(gkv.shape),
                  pl.BlockSpec((bm, 128), tab), pl.BlockSpec((bm, 128), tab)],
        out_specs=[pl.BlockSpec((bm, A_Q_LORA), row), pl.BlockSpec((bm, A_KV_LORA), row),
                   pl.BlockSpec((bm, 128), row)],
        out_shape=[jax.ShapeDtypeStruct((N, A_Q_LORA), BF16), jax.ShapeDtypeStruct((N, A_KV_LORA), BF16),
                   jax.ShapeDtypeStruct((N, 128), BF16)],
        compiler_params=_params("parallel"),
        name="mla_down",
    )(x, w, gq, gkv, cos_t, sin_t)


def _mla_q_up_kernel(cq_ref, w1_ref, w2_ref, cos_ref, sin_ref, q_ref):
    cq = cq_ref[...]
    a1 = jnp.dot(cq, w1_ref[...], preferred_element_type=F32)
    a2 = jnp.dot(cq, w2_ref[...], preferred_element_type=F32)
    q = a1 * _tile_lanes(cos_ref[...], A_HEADS) + a2 * _tile_lanes(sin_ref[...], A_HEADS)
    for h in range(A_HEADS):
        q_ref[0, h] = q[:, h * A_DK:(h + 1) * A_DK].astype(BF16)


def _mla_q_up(cq, w1, w2, cos_t, sin_t, B, S, bm):
    nsb = S // bm
    return pl.pallas_call(
        _mla_q_up_kernel,
        grid=(B * nsb,),
        in_specs=[pl.BlockSpec((bm, A_Q_LORA), lambda i: (i, 0)), _full(w1.shape), _full(w2.shape),
                  pl.BlockSpec((bm, 128), lambda i: (i % nsb, 0)), pl.BlockSpec((bm, 128), lambda i: (i % nsb, 0))],
        out_specs=pl.BlockSpec((1, A_HEADS, bm, A_DK), lambda i: (i // nsb, 0, i % nsb, 0)),
        out_shape=jax.ShapeDtypeStruct((B, A_HEADS, S, A_DK), BF16),
        compiler_params=_params("parallel"),
        name="mla_q_up",
    )(cq, w1, w2, cos_t, sin_t)


def _ones_rows(n):
    r = lax.broadcasted_iota(jnp.int32, (V_ROWS - 64, n), 0)
    return jnp.where(r == 0, 1.0, 0.0).astype(BF16)


def _mla_kv_up_kernel(ckv_ref, kr_ref, wk_ref, wvt_ref, k_ref, vt_ref):
    ckv = ckv_ref[...]
    ak = jnp.dot(ckv, wk_ref[...], preferred_element_type=F32)
    kr = kr_ref[...].astype(F32)
    avt = lax.dot_general(wvt_ref[...], ckv, NT, preferred_element_type=F32)
    ones = _ones_rows(ckv.shape[0])
    for h in range(A_HEADS):
        k_ref[0, h] = (ak[:, h * A_DK:(h + 1) * A_DK] + kr).astype(BF16)
        vt_ref[0, h, 0, 0:A_V, :] = avt[h * A_V:(h + 1) * A_V, :].astype(BF16)
        vt_ref[0, h, 0, A_V:V_ROWS, :] = ones


def _mla_kv_up(ckv, kr, wk, wvt, B, S, bm):
    nsb = S // bm
    return pl.pallas_call(
        _mla_kv_up_kernel,
        grid=(B * nsb,),
        in_specs=[pl.BlockSpec((bm, A_KV_LORA), lambda i: (i, 0)), pl.BlockSpec((bm, 128), lambda i: (i, 0)),
                  _full(wk.shape), _full(wvt.shape)],
        out_specs=[pl.BlockSpec((1, A_HEADS, bm, A_DK), lambda i: (i // nsb, 0, i % nsb, 0)),
                   pl.BlockSpec((1, A_HEADS, 1, V_ROWS, bm), lambda i: (i // nsb, 0, i % nsb, 0, 0))],
        out_shape=[jax.ShapeDtypeStruct((B, A_HEADS, S, A_DK), BF16),
                   jax.ShapeDtypeStruct((B, A_HEADS, nsb, V_ROWS, bm), BF16)],
        compiler_params=_params("parallel"),
        name="mla_kv_up",
    )(ckv, kr, wk, wvt)


def _gqa_proj_kernel(x_ref, wq_ref, wqp_ref, wk_ref, wvt_ref, ones_ref, gq_ref, gqp_ref, gk_ref, gkp_ref,
                     cos_ref, sin_ref, q_ref, k_ref, vt_ref):
    xb = x_ref[...].astype(BF16)
    nq = B_Q_HEADS * B_HEAD_DIM
    nk = B_KV_HEADS * B_HEAD_DIM
    cos = cos_ref[...]
    sin = sin_ref[...]
    ones_bd = ones_ref[...]

    def norm_rope(a, ap, g, gp, n):
        ss = jnp.dot((a * a).astype(BF16), ones_bd[:n, :n], preferred_element_type=F32) * (1.0 / B_HEAD_DIM)
        r = lax.rsqrt(ss + RMS_EPS)
        reps = n // 128
        return r * (a * g * _tile_lanes(cos, reps) + ap * gp * _tile_lanes(sin, reps))

    aq = jnp.dot(xb, wq_ref[...], preferred_element_type=F32)
    aqp = jnp.dot(xb, wqp_ref[...], preferred_element_type=F32)
    q = norm_rope(aq, aqp, gq_ref[...], gqp_ref[...], nq) * (B_HEAD_DIM ** -0.5 * LOG2E)
    akk = jnp.dot(xb, wk_ref[...], preferred_element_type=F32)
    k = norm_rope(akk[:, :nk], akk[:, nk:], gk_ref[...], gkp_ref[...], nk)
    avt = lax.dot_general(wvt_ref[...], xb, NT, preferred_element_type=F32)
    ones = _ones_rows(xb.shape[0])
    for h in range(B_Q_HEADS):
        q_ref[0, h] = q[:, h * B_HEAD_DIM:(h + 1) * B_HEAD_DIM].astype(BF16)
    for h in range(B_KV_HEADS):
        k_ref[0, h] = k[:, h * B_HEAD_DIM:(h + 1) * B_HEAD_DIM].astype(BF16)
        vt_ref[0, h, 0, 0:B_HEAD_DIM, :] = avt[h * B_HEAD_DIM:(h + 1) * B_HEAD_DIM, :].astype(BF16)
        vt_ref[0, h, 0, B_HEAD_DIM:V_ROWS, :] = ones


def _gqa_proj(x, wq, wqp, wk, wvt, ones_bd, gq, gqp, gk, gkp, cos_t, sin_t, B, S, bm):
    nsb = S // bm
    hm = lambda i: (i // nsb, 0, i % nsb, 0)
    tab = lambda i: (i % nsb, 0)
    ins = [x, wq, wqp, wk, wvt, ones_bd, gq, gqp, gk, gkp, cos_t, sin_t]
    specs = [pl.BlockSpec((bm, D_MODEL), lambda i: (i, 0))] + [_full(a.shape) for a in ins[1:10]]
    specs += [pl.BlockSpec((bm, 128), tab), pl.BlockSpec((bm, 128), tab)]
    return pl.pallas_call(
        _gqa_proj_kernel,
        grid=(B * nsb,),
        in_specs=specs,
        out_specs=[pl.BlockSpec((1, B_Q_HEADS, bm, B_HEAD_DIM), hm), pl.BlockSpec((1, B_KV_HEADS, bm, B_HEAD_DIM), hm),
                   pl.BlockSpec((1, B_KV_HEADS, 1, V_ROWS, bm), lambda i: (i // nsb, 0, i % nsb, 0, 0))],
        out_shape=[jax.ShapeDtypeStruct((B, B_Q_HEADS, S, B_HEAD_DIM), BF16),
                   jax.ShapeDtypeStruct((B, B_KV_HEADS, S, B_HEAD_DIM), BF16),
                   jax.ShapeDtypeStruct((B, B_KV_HEADS, nsb, V_ROWS, bm), BF16)],
        compiler_params=_params("parallel"),
        name="gqa_proj",
    )(*ins)


def _flash_kernel(q_ref, k_ref, vt_ref, o_ref, *scratch, hp, G, tq, tk, nk, dv, unroll, nbuf):
    s_bufs, acc_ref = scratch[:nbuf], scratch[nbuf]
    TQ = G * tq
    dk = q_ref.shape[-1]
    outs = []
    for h in range(hp):
        q = q_ref[0, h * G:(h + 1) * G].reshape(TQ, dk)

        def scores(c, s_ref, h=h, q=q):
            off = c * tk if isinstance(c, int) else pl.multiple_of(c * tk, tk)
            s = lax.dot_general(k_ref[0, h, pl.ds(off, tk), :], q, NT, preferred_element_type=F32)
            s_ref[...] = s
            return jnp.max(s, axis=0, keepdims=True)

        def step(c, i, m, a_cur, last, h=h):
            if not last:
                cm = scores(c + 1, s_bufs[(i + 1) % nbuf])
            p = jnp.exp2((s_bufs[i % nbuf][...] - m).astype(BF16))
            acc_ref[...] = a_cur * acc_ref[...] + jnp.dot(vt_ref[0, h, c], p, preferred_element_type=F32)
            if last:
                return m, a_cur
            m_new = jnp.maximum(m, cm)
            return m_new, jnp.exp2(m - m_new)

        def group(jj, carry, last=False):
            m, a_cur = carry
            for i in range(unroll):
                m, a_cur = step(unroll * jj + i, i, m, a_cur, last and i == unroll - 1)
            return m, a_cur

        acc_ref[...] = jnp.zeros_like(acc_ref)
        m0 = jnp.maximum(scores(0, s_bufs[0]), NEG)
        ng = nk // unroll
        carry = (m0, jnp.ones((1, TQ), F32))
        if ng > 1:
            carry = lax.fori_loop(0, ng - 1, group, carry)
        group(ng - 1, carry, last=True)
        acc = acc_ref[...]
        o = acc[:dv] * (1.0 / acc[dv:dv + 1])
        for g in range(G):
            outs.append(o[:, g * tq:(g + 1) * tq])
    o_ref[0] = jnp.concatenate(outs, axis=0).T.astype(o_ref.dtype)


def _flash(q, k, vt, *, hp, G, tq, nbuf, dv=64):
    B, H, S, dk = q.shape
    Hk = k.shape[1]
    nk, tk = vt.shape[2], vt.shape[4]
    unroll = 4 if nk % 4 == 0 else 2
    nbuf = min(nbuf, unroll)
    assert nk % unroll == 0 and unroll % nbuf == 0
    TQ = G * tq
    scratch = [pltpu.VMEM((tk, TQ), F32)] * nbuf + [pltpu.VMEM((V_ROWS, TQ), F32)]
    kern = functools.partial(_flash_kernel, hp=hp, G=G, tq=tq, tk=tk, nk=nk, dv=dv, unroll=unroll, nbuf=nbuf)
    return pl.pallas_call(
        kern,
        grid=(B, Hk // hp, S // tq),
        in_specs=[pl.BlockSpec((1, hp * G, tq, dk), lambda b, h, i: (b, h, i, 0)),
                  pl.BlockSpec((1, hp, S, dk), lambda b, h, i: (b, h, 0, 0)),
                  pl.BlockSpec((1, hp, nk, V_ROWS, tk), lambda b, h, i: (b, h, 0, 0, 0))],
        out_specs=pl.BlockSpec((1, tq, hp * G * dv), lambda b, h, i: (b, i, h)),
        out_shape=jax.ShapeDtypeStruct((B, S, H * dv), BF16),
        scratch_shapes=scratch,
        compiler_params=_params("parallel", "parallel", "arbitrary"),
        name="flash_attention",
    )(q, k, vt)


def _nbr_proj_kernel(x_ref, w_ref, wvt_ref, q_ref, k_ref, vt_ref):
    xb = x_ref[...].astype(BF16)
    n = C_HEADS * C_HEAD_DIM
    acc = jnp.dot(xb, w_ref[...], preferred_element_type=F32)
    avt = lax.dot_general(wvt_ref[...], xb, NT, preferred_element_type=F32)
    ones = _ones_rows(NBR_Q)
    for h in range(C_HEADS):
        lo = h * C_HEAD_DIM
        q_ref[0, h] = (acc[:, lo:lo + C_HEAD_DIM] * (C_HEAD_DIM ** -0.5)).astype(BF16)
        k_ref[0, h] = acc[:, n + lo:n + lo + C_HEAD_DIM].astype(BF16)
        for c in range(xb.shape[0] // NBR_Q):
            vt_ref[0, h, c, 0:C_HEAD_DIM, :] = avt[lo:lo + C_HEAD_DIM, c * NBR_Q:(c + 1) * NBR_Q].astype(BF16)
            vt_ref[0, h, c, C_HEAD_DIM:V_ROWS, :] = ones


def _nbr_proj(x, w, wvt, B, S, bm):
    nsb = S // bm
    hm = lambda i: (i // nsb, 0, i % nsb, 0)
    shp = jax.ShapeDtypeStruct((B, C_HEADS, S, C_HEAD_DIM), BF16)
    return pl.pallas_call(
        _nbr_proj_kernel,
        grid=(B * nsb,),
        in_specs=[pl.BlockSpec((bm, D_MODEL), lambda i: (i, 0)), _full(w.shape), _full(wvt.shape)],
        out_specs=[pl.BlockSpec((1, C_HEADS, bm, C_HEAD_DIM), hm)] * 2
        + [pl.BlockSpec((1, C_HEADS, bm // NBR_Q, V_ROWS, NBR_Q), lambda i: (i // nsb, 0, i % nsb, 0, 0))],
        out_shape=[shp, shp, jax.ShapeDtypeStruct((B, C_HEADS, S // NBR_Q, V_ROWS, NBR_Q), BF16)],
        compiler_params=_params("parallel"),
        name="nbr_proj",
    )(x, w, wvt)


def _nbr_kernel(var_ref, q_ref, *refs):
    del var_ref
    k_refs = refs[:NBR_KBLK]
    vt_refs = refs[NBR_KBLK:2 * NBR_KBLK]
    bias_ref, o_ref = refs[2 * NBR_KBLK], refs[2 * NBR_KBLK + 1]
    k = jnp.concatenate([r[0] for r in k_refs], axis=1)
    vt = jnp.concatenate([r[0, :, 0] for r in vt_refs], axis=2)
    s = jnp.einsum('hkd,hqd->hkq', k, q_ref[0], preferred_element_type=F32) + bias_ref[0]
    m = jnp.max(s, axis=1, keepdims=True)
    p = jnp.exp(s - m).astype(BF16)
    acc = jnp.einsum('hdk,hkq->hdq', vt, p, preferred_element_type=F32)
    o = acc[:, :C_HEAD_DIM] * (1.0 / acc[:, C_HEAD_DIM:C_HEAD_DIM + 1])
    o_ref[0] = o.reshape(C_HEADS * C_HEAD_DIM, NBR_Q).T.astype(o_ref.dtype)


def _nbr_windows(rows):
    nkr = 2 * NBR_KBLK
    sigs, var_of_u = [], []
    for u in range(rows // 2):
        ks = min(max(2 * u - C_WIN_H // 2, 0), rows - nkr)
        sig = []
        for a in range(2):
            r = 2 * u + a
            rs = min(max(r - C_WIN_H // 2, 0), rows - C_WIN_H)
            sig.append((ks - r, ks - rs))
        sig = tuple(sig)
        if sig not in sigs:
            sigs.append(sig)
        var_of_u.append(sigs.index(sig))
    row_idx = [[[min(max(i + dr + C_WIN_H - 1, 0), 2 * C_WIN_H - 2) for i in range(nkr)] for dr, _ in sig] for sig in sigs]
    inside = [[[0 <= i + ds < C_WIN_H for i in range(nkr)] for _, ds in sig] for sig in sigs]
    return var_of_u, row_idx, inside


def _nbr_bias(rpb, rows):
    kw, W = C_WIN_W, GRID_W
    var_of_u, row_idx, inside_row = _nbr_windows(rows)
    qc = jnp.arange(W)[:, None]
    kc = jnp.arange(W)[None, :]
    c_start = jnp.clip(qc - kw // 2, 0, W - kw)
    inside_col = (kc >= c_start) & (kc < c_start + kw)
    col_idx = jnp.clip(kc - qc + (kw - 1), 0, 2 * kw - 2)
    row_sel = jax.nn.one_hot(jnp.asarray(row_idx, jnp.int32), 2 * C_WIN_H - 1, dtype=F32)
    col_sel = jax.nn.one_hot(col_idx, 2 * kw - 1, dtype=F32)
    b = jnp.einsum('hrc,vair->hvaic', rpb, row_sel, precision=lax.Precision.HIGHEST)
    b = jnp.einsum('hvaic,qkc->hvaiqk', b, col_sel, precision=lax.Precision.HIGHEST)
    ok = jnp.asarray(inside_row)[None, :, :, :, None, None] & inside_col[None, None, None, None]
    b = jnp.where(ok, b, NEG).transpose(1, 0, 3, 5, 2, 4)
    nv = len(row_idx)
    return b.reshape(nv, C_HEADS, 2 * NBR_KBLK * W, NBR_Q).astype(F32), jnp.asarray(var_of_u, jnp.int32)


def _nbr_attention(q, k, vt, bias, var_of_u):
    B, H, S, d = q.shape
    nu = S // NBR_Q

    def first(u):
        return jnp.clip(u - C_WIN_H // 4, 0, nu - NBR_KBLK)

    k_specs = [pl.BlockSpec((1, H, NBR_Q, d), functools.partial(lambda b, u, var, i: (b, 0, first(u) + i, 0), i=i))
               for i in range(NBR_KBLK)]
    vt_specs = [pl.BlockSpec((1, H, 1, V_ROWS, NBR_Q), functools.partial(lambda b, u, var, i: (b, 0, first(u) + i, 0, 0), i=i))
                for i in range(NBR_KBLK)]
    grid_spec = pltpu.PrefetchScalarGridSpec(
        num_scalar_prefetch=1,
        grid=(B, nu),
        in_specs=[pl.BlockSpec((1, H, NBR_Q, d), lambda b, u, var: (b, 0, u, 0))] + k_specs + vt_specs
        + [pl.BlockSpec((1, H, NBR_KBLK * NBR_Q, NBR_Q), lambda b, u, var: (var[u], 0, 0, 0))],
        out_specs=pl.BlockSpec((1, NBR_Q, H * d), lambda b, u, var: (b, u, 0)),
    )
    return pl.pallas_call(
        _nbr_kernel,
        grid_spec=grid_spec,
        out_shape=jax.ShapeDtypeStruct((B, S, H * d), BF16),
        compiler_params=_params("parallel", "arbitrary"),
        name="nbr_attention",
    )(var_of_u, q, *([k] * NBR_KBLK), *([vt] * NBR_KBLK), bias)


def _out_ln_kernel(a_ref, w_ref, x_ref, g_ref, b_ref, o_ref):
    h = jnp.dot(a_ref[...], w_ref[...], preferred_element_type=F32)
    o_ref[...] = _layer_norm(DEEPNORM_ALPHA * x_ref[...] + h, g_ref[...], b_ref[...])


def _out_ln(a, w, x, g, b, bm):
    N = x.shape[0]
    row = lambda i: (i, 0)
    return pl.pallas_call(
        _out_ln_kernel,
        grid=(N // bm,),
        in_specs=[pl.BlockSpec((bm, a.shape[1]), row), _full(w.shape), pl.BlockSpec((bm, D_MODEL), row),
                  _full(g.shape), _full(b.shape)],
        out_specs=pl.BlockSpec((bm, D_MODEL), row),
        out_shape=jax.ShapeDtypeStruct((N, D_MODEL), F32),
        compiler_params=_params("parallel"),
        name="out_proj_ln",
    )(a, w, x, g, b)


def _swiglu_up_kernel(x_ref, wg_ref, wu_ref, h_ref):
    xb = x_ref[...].astype(BF16)
    g = jnp.dot(xb, wg_ref[...], preferred_element_type=F32)
    u = jnp.dot(xb, wu_ref[...], preferred_element_type=F32)
    h_ref[...] = (g * jax.nn.sigmoid(g) * u).astype(BF16)


def _swiglu_up(x, wg, wu, bm, bn):
    N = x.shape[0]
    F = wg.shape[1]
    return pl.pallas_call(
        _swiglu_up_kernel,
        grid=(F // bn, N // bm),
        in_specs=[pl.BlockSpec((bm, D_MODEL), lambda j, i: (i, 0)), pl.BlockSpec((D_MODEL, bn), lambda j, i: (0, j)),
                  pl.BlockSpec((D_MODEL, bn), lambda j, i: (0, j))],
        out_specs=pl.BlockSpec((bm, bn), lambda j, i: (i, j)),
        out_shape=jax.ShapeDtypeStruct((N, F), BF16),
        compiler_params=_params("parallel", "parallel"),
        name="swiglu_up",
    )(x, wg, wu)


def _ple(x, p_ref, wpg_ref, wpi_ref):
    gate = jax.nn.sigmoid(jnp.dot(x.astype(BF16), wpg_ref[...], preferred_element_type=F32))
    return gate * jnp.dot(p_ref[...].astype(BF16), wpi_ref[...], preferred_element_type=F32)


def _ffn_down_ln_kernel(h_ref, wd_ref, x_ref, p_ref, wpg_ref, wpi_ref, g_ref, b_ref, o_ref):
    x = x_ref[...]
    f = jnp.dot(h_ref[...], wd_ref[...], preferred_element_type=F32)
    o_ref[...] = _layer_norm(DEEPNORM_ALPHA * x + f + _ple(x, p_ref, wpg_ref, wpi_ref), g_ref[...], b_ref[...])


def _ffn_down_ln(h, wd, x, p, wpg, wpi, g, b, bm):
    N = x.shape[0]
    row = lambda i: (i, 0)
    return pl.pallas_call(
        _ffn_down_ln_kernel,
        grid=(N // bm,),
        in_specs=[pl.BlockSpec((bm, h.shape[1]), row), _full(wd.shape), pl.BlockSpec((bm, D_MODEL), row),
                  pl.BlockSpec((bm, PLE_DIM), row), _full(wpg.shape), _full(wpi.shape), _full(g.shape), _full(b.shape)],
        out_specs=pl.BlockSpec((bm, D_MODEL), row),
        out_shape=jax.ShapeDtypeStruct((N, D_MODEL), F32),
        compiler_params=_params("parallel"),
        name="ffn_down_ple_ln",
    )(h, wd, x, p, wpg, wpi, g, b)


def _router_kernel(x_ref, w_ref, idx_ref, gate_ref):
    logits = jnp.dot(x_ref[...], w_ref[...], preferred_element_type=F32, precision=lax.Precision.HIGHEST)
    lane = lax.broadcasted_iota(jnp.int32, logits.shape, 1).astype(F32)
    logits = jnp.where(lane < N_EXPERTS, logits, NEG)
    m1 = jnp.max(logits, axis=-1, keepdims=True)
    i1 = jnp.min(jnp.where(logits == m1, lane, 128.0), axis=-1, keepdims=True)
    rest = jnp.where(lane == i1, NEG, logits)
    m2 = jnp.max(rest, axis=-1, keepdims=True)
    i2 = jnp.min(jnp.where(rest == m2, lane, 128.0), axis=-1, keepdims=True)
    e = jnp.exp(m2 - m1)
    g1 = 1.0 / (1.0 + e)
    idx_ref[...] = jnp.where(lane == 0.0, i1, jnp.where(lane == 1.0, i2, 0.0)).astype(jnp.int32)
    gate_ref[...] = jnp.where(lane == 0.0, g1, jnp.where(lane == 1.0, e * g1, 0.0))


def _router(x, w, bm):
    N = x.shape[0]
    row = lambda i: (i, 0)
    return pl.pallas_call(
        _router_kernel,
        grid=(N // bm,),
        in_specs=[pl.BlockSpec((bm, D_MODEL), row), _full(w.shape)],
        out_specs=[pl.BlockSpec((bm, 128), row)] * 2,
        out_shape=[jax.ShapeDtypeStruct((N, 128), jnp.int32), jax.ShapeDtypeStruct((N, 128), F32)],
        compiler_params=_params("parallel"),
        name="moe_router",
    )(x, w)


def _moe_ffn_kernel(te_ref, nt_ref, xs_ref, wg_ref, wu_ref, wd_ref, o_ref, acc_ref):
    t = pl.program_id(0)
    j = pl.program_id(1)

    @pl.when(t < nt_ref[0])
    def _():
        xs = xs_ref[...]
        g = jnp.dot(xs, wg_ref[0], preferred_element_type=F32)
        u = jnp.dot(xs, wu_ref[0], preferred_element_type=F32)
        h = (g * jax.nn.sigmoid(g) * u).astype(BF16)
        y = jnp.dot(h, wd_ref[0], preferred_element_type=F32)

        @pl.when(j == 0)
        def _():
            acc_ref[...] = y

        @pl.when(j == pl.num_programs(1) - 1)
        def _():
            o_ref[...] = (acc_ref[...] + y).astype(o_ref.dtype)

    @pl.when(t >= nt_ref[0])
    def _():
        o_ref[...] = jnp.zeros_like(o_ref)


def _moe_ffn(tile_expert, n_tiles, xs, wg, wu, wd, tm, halves=2):
    P = xs.shape[0]
    fh = FF_EXPERT // halves
    grid_spec = pltpu.PrefetchScalarGridSpec(
        num_scalar_prefetch=2,
        grid=(P // tm, halves),
        in_specs=[pl.BlockSpec((tm, D_MODEL), lambda t, j, te, nt: (t, 0)),
                  pl.BlockSpec((1, D_MODEL, fh), lambda t, j, te, nt: (te[t], 0, j)),
                  pl.BlockSpec((1, D_MODEL, fh), lambda t, j, te, nt: (te[t], 0, j)),
                  pl.BlockSpec((1, fh, D_MODEL), lambda t, j, te, nt: (te[t], j, 0))],
        out_specs=pl.BlockSpec((tm, D_MODEL), lambda t, j, te, nt: (t, 0)),
        scratch_shapes=[pltpu.VMEM((tm, D_MODEL), F32)],
    )
    return pl.pallas_call(
        _moe_ffn_kernel,
        grid_spec=grid_spec,
        out_shape=jax.ShapeDtypeStruct((P, D_MODEL), BF16),
        compiler_params=_params("arbitrary", "arbitrary"),
        name="moe_expert_ffn",
    )(tile_expert, n_tiles, xs, wg, wu, wd)


def _moe_combine_ln_kernel(ya_ref, yb_ref, gate_ref, x_ref, p_ref, wpg_ref, wpi_ref, g_ref, b_ref, o_ref):
    x = x_ref[...]
    gate = gate_ref[...]
    f = gate[:, 0:1] * ya_ref[...] + gate[:, 1:2] * yb_ref[...]
    o_ref[...] = _layer_norm(DEEPNORM_ALPHA * x + f + _ple(x, p_ref, wpg_ref, wpi_ref), g_ref[...], b_ref[...])


def _moe_combine_ln(ya, yb, gates, x, p, wpg, wpi, g, b, bm):
    N = x.shape[0]
    row = lambda i: (i, 0)
    rows = pl.BlockSpec((bm, D_MODEL), row)
    return pl.pallas_call(
        _moe_combine_ln_kernel,
        grid=(N // bm,),
        in_specs=[rows, rows, pl.BlockSpec((bm, 128), row), rows, pl.BlockSpec((bm, PLE_DIM), row),
                  _full(wpg.shape), _full(wpi.shape), _full(g.shape), _full(b.shape)],
        out_specs=rows,
        out_shape=jax.ShapeDtypeStruct((N, D_MODEL), F32),
        compiler_params=_params("parallel"),
        name="moe_combine_ple_ln",
    )(ya, yb, gates, x, p, wpg, wpi, g, b)


def _route(idx, tm):
    N = idx.shape[0]
    e_flat = idx.reshape(-1)
    onehot = (e_flat[:, None] == jnp.arange(N_EXPERTS, dtype=jnp.int32)[None, :]).astype(jnp.int32)
    csum = jnp.cumsum(onehot, axis=0)
    rank = jnp.sum(csum * onehot, axis=1) - 1
    counts = csum[-1]
    padded = ((counts + tm - 1) // tm) * tm
    ends = jnp.cumsum(padded)
    slot = (ends - padded)[e_flat] + rank
    P = TOP_K * N + N_EXPERTS * tm
    row_token = jnp.zeros((P,), jnp.int32).at[slot].set(jnp.arange(TOP_K * N, dtype=jnp.int32) // TOP_K)
    tile_start = jnp.arange(P // tm, dtype=jnp.int32) * tm
    tile_expert = jnp.minimum(jnp.sum((tile_start[:, None] >= ends[None, :]).astype(jnp.int32), axis=1), N_EXPERTS - 1)
    n_tiles = (ends[-1] // tm).astype(jnp.int32).reshape(1)
    return slot.reshape(N, TOP_K), row_token, tile_expert.astype(jnp.int32), n_tiles


def _swap_halves(w, group):
    shp = w.shape
    w = w.reshape(shp[:-1] + (shp[-1] // group, 2, group // 2))
    return jnp.flip(w, axis=-2).reshape(shp)


def _rope_tables(pos, dim):
    inv = ROPE_THETA ** (-jnp.arange(0, dim, 2, dtype=F32) / dim)
    ang = pos.astype(F32)[:, None] * inv[None, :]
    c, s = jnp.cos(ang), jnp.sin(ang)
    return jnp.concatenate([c, c], -1), jnp.concatenate([-s, s], -1)


def _mla_prep(w_dq, w_dkv, w_uq, w_ukv, S):
    wd = jnp.zeros((D_MODEL, 896), F32)
    wd = wd.at[:, :A_Q_LORA].set(w_dq).at[:, A_Q_LORA:640].set(w_dkv[:, :A_KV_LORA])
    w_r = w_dkv[:, A_KV_LORA:]
    wd = wd.at[:, 640 + A_NOPE:640 + A_NOPE + A_ROPE].set(w_r)
    wd = wd.at[:, 768 + A_NOPE:768 + A_NOPE + A_ROPE].set(_swap_halves(w_r, A_ROPE))
    uq = w_uq.reshape(A_Q_LORA, A_HEADS, A_NOPE + A_ROPE)
    w1 = jnp.pad(uq, ((0, 0), (0, 0), (0, A_DK - A_NOPE - A_ROPE))).reshape(A_Q_LORA, A_HEADS * A_DK)
    w2 = jnp.pad(_swap_halves(uq[..., A_NOPE:], A_ROPE), ((0, 0), (0, 0), (A_NOPE, A_DK - A_NOPE - A_ROPE)))
    w2 = w2.reshape(A_Q_LORA, A_HEADS * A_DK)
    ukv = w_ukv.reshape(A_KV_LORA, A_HEADS, A_NOPE + A_V)
    wk = jnp.pad(ukv[..., :A_NOPE], ((0, 0), (0, 0), (0, A_DK - A_NOPE))).reshape(A_KV_LORA, A_HEADS * A_DK)
    wvt = ukv[..., A_NOPE:].reshape(A_KV_LORA, A_HEADS * A_V).T
    c, s = _rope_tables(jnp.arange(S), A_ROPE)
    pad = lambda t, fill: jnp.concatenate([jnp.full((S, A_NOPE), fill, F32), t, jnp.zeros((S, A_DK - A_NOPE - A_ROPE), F32)], -1)
    scale = (A_NOPE + A_ROPE) ** -0.5 * LOG2E
    tabs = dict(k_cos=pad(c, 0.0), k_sin=pad(s, 0.0), q_cos=pad(c, 1.0) * scale, q_sin=pad(s, 0.0) * scale)
    return wd.astype(BF16), w1.astype(BF16), w2.astype(BF16), wk.astype(BF16), wvt.astype(BF16), tabs


def _gqa_prep(w_qkv, g_q, g_k, S):
    nq = B_Q_HEADS * B_HEAD_DIM
    nk = B_KV_HEADS * B_HEAD_DIM
    half = B_HEAD_DIM // 2
    wq, wk, wv = w_qkv[:, :nq], w_qkv[:, nq:nq + nk], w_qkv[:, nq + nk:]
    wkk = jnp.concatenate([wk, _swap_halves(wk, half)], axis=1)
    t = jnp.arange(S)
    cr, sr = _rope_tables(t // GRID_W, half)
    cc, sc = _rope_tables(t % GRID_W, half)
    cos_t = jnp.tile(jnp.concatenate([cr, cc], -1), (1, 2))
    sin_t = jnp.tile(jnp.concatenate([sr, sc], -1), (1, 2))
    lane = jnp.arange(nq) // B_HEAD_DIM
    ones_bd = (lane[:, None] == lane[None, :]).astype(BF16)
    gq = jnp.tile(g_q, B_Q_HEADS)[None, :]
    gk = jnp.tile(g_k, B_KV_HEADS)[None, :]
    return (wq.astype(BF16), _swap_halves(wq, half).astype(BF16), wkk.astype(BF16), wv.T.astype(BF16), ones_bd,
            gq, _swap_halves(gq, half), gk, _swap_halves(gk, half), cos_t, sin_t)


def _trunk(x, p, w):
    B, S, _ = x.shape
    N = B * S
    bm = min(ROW_BLOCK, S)
    tq = min(512, S)
    x = x.reshape(N, D_MODEL)
    for i in range(DEPTH):
        kind, j = i % N_MIXERS, i // N_MIXERS
        if kind == 0:
            wd, w1, w2, wk, wvt, tabs = _mla_prep(w['a_w_dq'][j], w['a_w_dkv'][j], w['a_w_uq'][j], w['a_w_ukv'][j], S)
            cq, ckv, kr = _mla_down(x, wd, w['a_g_q'][j][None, :], w['a_g_kv'][j][None, :], tabs['k_cos'], tabs['k_sin'], S, bm)
            q = _mla_q_up(cq, w1, w2, tabs['q_cos'], tabs['q_sin'], B, S, bm)
            k, vt = _mla_kv_up(ckv, kr, wk, wvt, B, S, min(KV_CHUNK, S))
            a = _flash(q, k, vt, hp=2, G=1, tq=tq, nbuf=4)
            w_o = w['a_w_o'][j]
        elif kind == 1:
            prep = _gqa_prep(w['b_w_qkv'][j], w['b_g_q'][j], w['b_g_k'][j], S)
            q, k, vt = _gqa_proj(x, *prep, B, S, bm)
            a = _flash(q, k, vt, hp=1, G=B_Q_HEADS // B_KV_HEADS, tq=min(256, S), nbuf=2)
            w_o = w['b_w_o'][j]
        else:
            n = C_HEADS * C_HEAD_DIM
            w_qkv = w['c_w_qkv'][j]
            q, k, vt = _nbr_proj(x, w_qkv[:, :2 * n].astype(BF16), w_qkv[:, 2 * n:].T.astype(BF16), B, S, bm)
            a = _nbr_attention(q, k, vt, *_nbr_bias(w['c_rpb'][j], S // GRID_W))
            w_o = w['c_w_o'][j]
        x = _out_ln(a.reshape(N, D_MODEL), w_o.astype(BF16), x, w['ln1_g'][i][None, :], w['ln1_b'][i][None, :], bm)
        f_i = i // 2
        pi = p[i].reshape(N, PLE_DIM)
        wpg, wpi = w['ple_w_gate'][i].astype(BF16), w['ple_w_in'][i].astype(BF16)
        g2, b2 = w['ln2_g'][i][None, :], w['ln2_b'][i][None, :]
        if i % 2 == 0:
            h = _swiglu_up(x, w['f_w_gate'][f_i].astype(BF16), w['f_w_up'][f_i].astype(BF16), bm, FF_DENSE // 2)
            x = _ffn_down_ln(h, w['f_w_down'][f_i].astype(BF16), x, pi, wpg, wpi, g2, b2, bm)
        else:
            w_r = jnp.pad(w['m_w_router'][f_i], ((0, 0), (0, 128 - N_EXPERTS)))
            idx, gates = _router(x, w_r, bm)
            slot, row_token, tile_expert, n_tiles = _route(idx[:, :TOP_K], MOE_TILE)
            xs = jnp.take(x.astype(BF16), row_token, axis=0)
            ys = _moe_ffn(tile_expert, n_tiles, xs, w['m_w_gate'][f_i].astype(BF16), w['m_w_up'][f_i].astype(BF16),
                          w['m_w_down'][f_i].astype(BF16), MOE_TILE)
            ya = jnp.take(ys, slot[:, 0], axis=0)
            yb = jnp.take(ys, slot[:, 1], axis=0)
            x = _moe_combine_ln(ya, yb, gates, x, pi, wpg, wpi, g2, b2, bm)
    return x.reshape(B, S, D_MODEL)


def kernel(x_prompt, x_sample, p_prompt, p_sample, a_w_dq, a_g_q, a_w_uq, a_w_dkv, a_g_kv, a_w_ukv, a_w_o, b_w_qkv, b_g_q, b_g_k, b_w_o, c_w_qkv, c_rpb, c_w_o, ln1_g, ln1_b, ln2_g, ln2_b, f_w_gate, f_w_up, f_w_down, m_w_router, m_w_gate, m_w_up, m_w_down, ple_w_gate, ple_w_in):
    w = dict(a_w_dq=a_w_dq, a_g_q=a_g_q, a_w_uq=a_w_uq, a_w_dkv=a_w_dkv, a_g_kv=a_g_kv, a_w_ukv=a_w_ukv, a_w_o=a_w_o,
             b_w_qkv=b_w_qkv, b_g_q=b_g_q, b_g_k=b_g_k, b_w_o=b_w_o, c_w_qkv=c_w_qkv, c_rpb=c_rpb, c_w_o=c_w_o,
             ln1_g=ln1_g, ln1_b=ln1_b, ln2_g=ln2_g, ln2_b=ln2_b, f_w_gate=f_w_gate, f_w_up=f_w_up, f_w_down=f_w_down,
             m_w_router=m_w_router, m_w_gate=m_w_gate, m_w_up=m_w_up, m_w_down=m_w_down,
             ple_w_gate=ple_w_gate, ple_w_in=ple_w_in)
    return (_trunk(x_prompt, p_prompt, w), _trunk(x_sample, p_sample, w))
```

```python
import functools

import jax
import jax.numpy as jnp
from jax import lax
from jax.experimental import pallas as pl
from jax.experimental.pallas import tpu as pltpu

F32 = jnp.float32
BF16 = jnp.bfloat16

D_MODEL = 1024
DEPTH = 4
GRID_W = 64
PLE_DIM = 256
N_MIXERS = 3
ROPE_THETA = 10000.0
RMS_EPS = 1e-6
LN_EPS = 1e-5
DEEPNORM_ALPHA = (2 * DEPTH) ** 0.25

A_HEADS = 16
A_Q_LORA = 384
A_KV_LORA = 256
A_NOPE = 64
A_ROPE = 32
A_V = 64
A_DK = 128
B_Q_HEADS = 16
B_KV_HEADS = 4
B_HEAD_DIM = 64
C_HEADS = 16
C_HEAD_DIM = 64
C_WIN_H = 8
C_WIN_W = 16
FF_DENSE = 2816
N_EXPERTS = 8
TOP_K = 2
FF_EXPERT = 3584

V_ROWS = 80
NEG = -1e30
LOG2E = 1.4426950408889634
VMEM_LIMIT = 56 * 1024 * 1024
ROW_BLOCK = 512
KV_CHUNK = 512
MOE_TILE = 512
NBR_Q = 2 * GRID_W
NBR_KBLK = (C_WIN_H + 2) // 2
NT = (((1,), (1,)), ((), ()))


def _params(*sem):
    return pltpu.CompilerParams(dimension_semantics=sem, vmem_limit_bytes=VMEM_LIMIT)


def _full(shape):
    return pl.BlockSpec(shape, lambda *_: (0,) * len(shape))


def _rms(x, g):
    return x * lax.rsqrt(jnp.mean(x * x, axis=-1, keepdims=True) + RMS_EPS) * g


def _layer_norm(x, g, b):
    mu = jnp.mean(x, axis=-1, keepdims=True)
    xc = x - mu
    var = jnp.mean(xc * xc, axis=-1, keepdims=True)
    return xc * lax.rsqrt(var + LN_EPS) * g + b


def _tile_lanes(t, n):
    return jnp.concatenate([t] * n, axis=-1)


def _mla_down_kernel(x_ref, w_ref, gq_ref, gkv_ref, cos_ref, sin_ref, cq_ref, ckv_ref, kr_ref):
    acc = jnp.dot(x_ref[...].astype(BF16), w_ref[...], preferred_element_type=F32)
    cq_ref[...] = _rms(acc[:, :A_Q_LORA], gq_ref[...]).astype(BF16)
    ckv_ref[...] = _rms(acc[:, A_Q_LORA:640], gkv_ref[...]).astype(BF16)
    kr_ref[...] = (acc[:, 640:768] * cos_ref[...] + acc[:, 768:896] * sin_ref[...]).astype(BF16)


def _mla_down(x, w, gq, gkv, cos_t, sin_t, S, bm):
    N = x.shape[0]
    nsb = S // bm
    row = lambda i: (i, 0)
    tab = lambda i: (i % nsb, 0)
    return pl.pallas_call(
        _mla_down_kernel,
        grid=(N // bm,),
        in_specs=[pl.BlockSpec((bm, D_MODEL), row), _full(w.shape), _full(gq.shape), _full(gkv.shape),
                  pl.BlockSpec((bm, 128), tab), pl.BlockSpec((bm, 128), tab)],
        out_specs=[pl.BlockSpec((bm, A_Q_LORA), row), pl.BlockSpec((bm, A_KV_LORA), row),
                   pl.BlockSpec((bm, 128), row)],
        out_shape=[jax.ShapeDtypeStruct((N, A_Q_LORA), BF16), jax.ShapeDtypeStruct((N, A_KV_LORA), BF16),
                   jax.ShapeDtypeStruct((N, 128), BF16)],
        compiler_params=_params("parallel"),
        name="mla_down",
    )(x, w, gq, gkv, cos_t, sin_t)


def _mla_q_up_kernel(cq_ref, w1_ref, w2_ref, cos_ref, sin_ref, q_ref):
    cq = cq_ref[...]
    a1 = jnp.dot(cq, w1_ref[...], preferred_element_type=F32)
    a2 = jnp.dot(cq, w2_ref[...], preferred_element_type=F32)
    q = a1 * _tile_lanes(cos_ref[...], A_HEADS) + a2 * _tile_lanes(sin_ref[...], A_HEADS)
    for h in range(A_HEADS):
        q_ref[0, h] = q[:, h * A_DK:(h + 1) * A_DK].astype(BF16)


def _mla_q_up(cq, w1, w2, cos_t, sin_t, B, S, bm):
    nsb = S // bm
    return pl.pallas_call(
        _mla_q_up_kernel,
        grid=(B * nsb,),
        in_specs=[pl.BlockSpec((bm, A_Q_LORA), lambda i: (i, 0)), _full(w1.shape), _full(w2.shape),
                  pl.BlockSpec((bm, 128), lambda i: (i % nsb, 0)), pl.BlockSpec((bm, 128), lambda i: (i % nsb, 0))],
        out_specs=pl.BlockSpec((1, A_HEADS, bm, A_DK), lambda i: (i // nsb, 0, i % nsb, 0)),
        out_shape=jax.ShapeDtypeStruct((B, A_HEADS, S, A_DK), BF16),
        compiler_params=_params("parallel"),
        name="mla_q_up",
    )(cq, w1, w2, cos_t, sin_t)


def _ones_rows(n):
    r = lax.broadcasted_iota(jnp.int32, (V_ROWS - 64, n), 0)
    return jnp.where(r == 0, 1.0, 0.0).astype(BF16)


def _mla_kv_up_kernel(ckv_ref, kr_ref, wk_ref, wvt_ref, k_ref, vt_ref):
    ckv = ckv_ref[...]
    ak = jnp.dot(ckv, wk_ref[...], preferred_element_type=F32)
    kr = kr_ref[...].astype(F32)
    avt = lax.dot_general(wvt_ref[...], ckv, NT, preferred_element_type=F32)
    ones = _ones_rows(ckv.shape[0])
    for h in range(A_HEADS):
        k_ref[0, h] = (ak[:, h * A_DK:(h + 1) * A_DK] + kr).astype(BF16)
        vt_ref[0, h, 0, 0:A_V, :] = avt[h * A_V:(h + 1) * A_V, :].astype(BF16)
        vt_ref[0, h, 0, A_V:V_ROWS, :] = ones


def _mla_kv_up(ckv, kr, wk, wvt, B, S, bm):
    nsb = S // bm
    return pl.pallas_call(
        _mla_kv_up_kernel,
        grid=(B * nsb,),
        in_specs=[pl.BlockSpec((bm, A_KV_LORA), lambda i: (i, 0)), pl.BlockSpec((bm, 128), lambda i: (i, 0)),
                  _full(wk.shape), _full(wvt.shape)],
        out_specs=[pl.BlockSpec((1, A_HEADS, bm, A_DK), lambda i: (i // nsb, 0, i % nsb, 0)),
                   pl.BlockSpec((1, A_HEADS, 1, V_ROWS, bm), lambda i: (i // nsb, 0, i % nsb, 0, 0))],
        out_shape=[jax.ShapeDtypeStruct((B, A_HEADS, S, A_DK), BF16),
                   jax.ShapeDtypeStruct((B, A_HEADS, nsb, V_ROWS, bm), BF16)],
        compiler_params=_params("parallel"),
        name="mla_kv_up",
    )(ckv, kr, wk, wvt)


def _gqa_proj_kernel(x_ref, wq_ref, wqp_ref, wk_ref, wvt_ref, ones_ref, gq_ref, gqp_ref, gk_ref, gkp_ref,
                     cos_ref, sin_ref, q_ref, k_ref, vt_ref):
    xb = x_ref[...].astype(BF16)
    nq = B_Q_HEADS * B_HEAD_DIM
    nk = B_KV_HEADS * B_HEAD_DIM
    cos = cos_ref[...]
    sin = sin_ref[...]
    ones_bd = ones_ref[...]

    def norm_rope(a, ap, g, gp, n):
        ss = jnp.dot((a * a).astype(BF16), ones_bd[:n, :n], preferred_element_type=F32) * (1.0 / B_HEAD_DIM)
        r = lax.rsqrt(ss + RMS_EPS)
        reps = n // 128
        return r * (a * g * _tile_lanes(cos, reps) + ap * gp * _tile_lanes(sin, reps))

    aq = jnp.dot(xb, wq_ref[...], preferred_element_type=F32)
    aqp = jnp.dot(xb, wqp_ref[...], preferred_element_type=F32)
    q = norm_rope(aq, aqp, gq_ref[...], gqp_ref[...], nq) * (B_HEAD_DIM ** -0.5 * LOG2E)
    akk = jnp.dot(xb, wk_ref[...], preferred_element_type=F32)
    k = norm_rope(akk[:, :nk], akk[:, nk:], gk_ref[...], gkp_ref[...], nk)
    avt = lax.dot_general(wvt_ref[...], xb, NT, preferred_element_type=F32)
    ones = _ones_rows(xb.shape[0])
    for h in range(B_Q_HEADS):
        q_ref[0, h] = q[:, h * B_HEAD_DIM:(h + 1) * B_HEAD_DIM].astype(BF16)
    for h in range(B_KV_HEADS):
        k_ref[0, h] = k[:, h * B_HEAD_DIM:(h + 1) * B_HEAD_DIM].astype(BF16)
        vt_ref[0, h, 0, 0:B_HEAD_DIM, :] = avt[h * B_HEAD_DIM:(h + 1) * B_HEAD_DIM, :].astype(BF16)
        vt_ref[0, h, 0, B_HEAD_DIM:V_ROWS, :] = ones


def _gqa_proj(x, wq, wqp, wk, wvt, ones_bd, gq, gqp, gk, gkp, cos_t, sin_t, B, S, bm):
    nsb = S // bm
    hm = lambda i: (i // nsb, 0, i % nsb, 0)
    tab = lambda i: (i % nsb, 0)
    ins = [x, wq, wqp, wk, wvt, ones_bd, gq, gqp, gk, gkp, cos_t, sin_t]
    specs = [pl.BlockSpec((bm, D_MODEL), lambda i: (i, 0))] + [_full(a.shape) for a in ins[1:10]]
    specs += [pl.BlockSpec((bm, 128), tab), pl.BlockSpec((bm, 128), tab)]
    return pl.pallas_call(
        _gqa_proj_kernel,
        grid=(B * nsb,),
        in_specs=specs,
        out_specs=[pl.BlockSpec((1, B_Q_HEADS, bm, B_HEAD_DIM), hm), pl.BlockSpec((1, B_KV_HEADS, bm, B_HEAD_DIM), hm),
                   pl.BlockSpec((1, B_KV_HEADS, 1, V_ROWS, bm), lambda i: (i // nsb, 0, i % nsb, 0, 0))],
        out_shape=[jax.ShapeDtypeStruct((B, B_Q_HEADS, S, B_HEAD_DIM), BF16),
                   jax.ShapeDtypeStruct((B, B_KV_HEADS, S, B_HEAD_DIM), BF16),
                   jax.ShapeDtypeStruct((B, B_KV_HEADS, nsb, V_ROWS, bm), BF16)],
        compiler_params=_params("parallel"),
        name="gqa_proj",
    )(*ins)


def _flash_kernel(q_ref, k_ref, vt_ref, o_ref, *scratch, hp, G, tq, tk, nk, dv, unroll, nbuf):
    s_bufs, acc_ref = scratch[:nbuf], scratch[nbuf]
    TQ = G * tq
    dk = q_ref.shape[-1]
    outs = []
    for h in range(hp):
        q = q_ref[0, h * G:(h + 1) * G].reshape(TQ, dk)

        def scores(c, s_ref, h=h, q=q):
            off = c * tk if isinstance(c, int) else pl.multiple_of(c * tk, tk)
            s = lax.dot_general(k_ref[0, h, pl.ds(off, tk), :], q, NT, preferred_element_type=F32)
            s_ref[...] = s
            return jnp.max(s, axis=0, keepdims=True)

        def step(c, i, m, a_cur, last, h=h):
            if not last:
                cm = scores(c + 1, s_bufs[(i + 1) % nbuf])
            p = jnp.exp2((s_bufs[i % nbuf][...] - m).astype(BF16))
            acc_ref[...] = a_cur * acc_ref[...] + jnp.dot(vt_ref[0, h, c], p, preferred_element_type=F32)
            if last:
                return m, a_cur
            m_new = jnp.maximum(m, cm)
            return m_new, jnp.exp2(m - m_new)

        def group(jj, carry, last=False):
            m, a_cur = carry
            for i in range(unroll):
                m, a_cur = step(unroll * jj + i, i, m, a_cur, last and i == unroll - 1)
            return m, a_cur

        acc_ref[...] = jnp.zeros_like(acc_ref)
        m0 = jnp.maximum(scores(0, s_bufs[0]), NEG)
        ng = nk // unroll
        carry = (m0, jnp.ones((1, TQ), F32))
        if ng > 1:
            carry = lax.fori_loop(0, ng - 1, group, carry)
        group(ng - 1, carry, last=True)
        acc = acc_ref[...]
        o = acc[:dv] * (1.0 / acc[dv:dv + 1])
        for g in range(G):
            outs.append(o[:, g * tq:(g + 1) * tq])
    o_ref[0] = jnp.concatenate(outs, axis=0).T.astype(o_ref.dtype)


def _flash(q, k, vt, *, hp, G, tq, nbuf, dv=64):
    B, H, S, dk = q.shape
    Hk = k.shape[1]
    nk, tk = vt.shape[2], vt.shape[4]
    unroll = 8 if nk % 8 == 0 and nk > 8 else 4 if nk % 4 == 0 else 2
    nbuf = min(nbuf, unroll)
    assert nk % unroll == 0 and unroll % nbuf == 0
    TQ = G * tq
    scratch = [pltpu.VMEM((tk, TQ), F32)] * nbuf + [pltpu.VMEM((V_ROWS, TQ), F32)]
    kern = functools.partial(_flash_kernel, hp=hp, G=G, tq=tq, tk=tk, nk=nk, dv=dv, unroll=unroll, nbuf=nbuf)
    return pl.pallas_call(
        kern,
        grid=(B, Hk // hp, S // tq),
        in_specs=[pl.BlockSpec((1, hp * G, tq, dk), lambda b, h, i: (b, h, i, 0)),
                  pl.BlockSpec((1, hp, S, dk), lambda b, h, i: (b, h, 0, 0)),
                  pl.BlockSpec((1, hp, nk, V_ROWS, tk), lambda b, h, i: (b, h, 0, 0, 0))],
        out_specs=pl.BlockSpec((1, tq, hp * G * dv), lambda b, h, i: (b, i, h)),
        out_shape=jax.ShapeDtypeStruct((B, S, H * dv), BF16),
        scratch_shapes=scratch,
        compiler_params=_params("parallel", "parallel", "arbitrary"),
        name="flash_attention",
    )(q, k, vt)


def _nbr_proj_kernel(x_ref, w_ref, wvt_ref, q_ref, k_ref, vt_ref):
    xb = x_ref[...].astype(BF16)
    n = C_HEADS * C_HEAD_DIM
    acc = jnp.dot(xb, w_ref[...], preferred_element_type=F32)
    avt = lax.dot_general(wvt_ref[...], xb, NT, preferred_element_type=F32)
    ones = _ones_rows(NBR_Q)
    for h in range(C_HEADS):
        lo = h * C_HEAD_DIM
        q_ref[0, h] = (acc[:, lo:lo + C_HEAD_DIM] * (C_HEAD_DIM ** -0.5)).astype(BF16)
        k_ref[0, h] = acc[:, n + lo:n + lo + C_HEAD_DIM].astype(BF16)
        for c in range(xb.shape[0] // NBR_Q):
            vt_ref[0, h, c, 0:C_HEAD_DIM, :] = avt[lo:lo + C_HEAD_DIM, c * NBR_Q:(c + 1) * NBR_Q].astype(BF16)
            vt_ref[0, h, c, C_HEAD_DIM:V_ROWS, :] = ones


def _nbr_proj(x, w, wvt, B, S, bm):
    nsb = S // bm
    hm = lambda i: (i // nsb, 0, i % nsb, 0)
    shp = jax.ShapeDtypeStruct((B, C_HEADS, S, C_HEAD_DIM), BF16)
    return pl.pallas_call(
        _nbr_proj_kernel,
        grid=(B * nsb,),
        in_specs=[pl.BlockSpec((bm, D_MODEL), lambda i: (i, 0)), _full(w.shape), _full(wvt.shape)],
        out_specs=[pl.BlockSpec((1, C_HEADS, bm, C_HEAD_DIM), hm)] * 2
        + [pl.BlockSpec((1, C_HEADS, bm // NBR_Q, V_ROWS, NBR_Q), lambda i: (i // nsb, 0, i % nsb, 0, 0))],
        out_shape=[shp, shp, jax.ShapeDtypeStruct((B, C_HEADS, S // NBR_Q, V_ROWS, NBR_Q), BF16)],
        compiler_params=_params("parallel"),
        name="nbr_proj",
    )(x, w, wvt)


def _nbr_kernel(var_ref, q_ref, *refs):
    del var_ref
    k_refs = refs[:NBR_KBLK]
    vt_refs = refs[NBR_KBLK:2 * NBR_KBLK]
    bias_ref, o_ref = refs[2 * NBR_KBLK], refs[2 * NBR_KBLK + 1]
    k = jnp.concatenate([r[0] for r in k_refs], axis=1)
    vt = jnp.concatenate([r[0, :, 0] for r in vt_refs], axis=2)
    s = jnp.einsum('hkd,hqd->hkq', k, q_ref[0], preferred_element_type=F32) + bias_ref[0]
    m = jnp.max(s, axis=1, keepdims=True)
    p = jnp.exp(s - m).astype(BF16)
    acc = jnp.einsum('hdk,hkq->hdq', vt, p, preferred_element_type=F32)
    o = acc[:, :C_HEAD_DIM] * (1.0 / acc[:, C_HEAD_DIM:C_HEAD_DIM + 1])
    o_ref[0] = o.reshape(C_HEADS * C_HEAD_DIM, NBR_Q).T.astype(o_ref.dtype)


def _nbr_windows(rows):
    nkr = 2 * NBR_KBLK
    sigs, var_of_u = [], []
    for u in range(rows // 2):
        ks = min(max(2 * u - C_WIN_H // 2, 0), rows - nkr)
        sig = []
        for a in range(2):
            r = 2 * u + a
            rs = min(max(r - C_WIN_H // 2, 0), rows - C_WIN_H)
            sig.append((ks - r, ks - rs))
        sig = tuple(sig)
        if sig not in sigs:
            sigs.append(sig)
        var_of_u.append(sigs.index(sig))
    row_idx = [[[min(max(i + dr + C_WIN_H - 1, 0), 2 * C_WIN_H - 2) for i in range(nkr)] for dr, _ in sig] for sig in sigs]
    inside = [[[0 <= i + ds < C_WIN_H for i in range(nkr)] for _, ds in sig] for sig in sigs]
    return var_of_u, row_idx, inside


def _nbr_bias(rpb, rows):
    kw, W = C_WIN_W, GRID_W
    var_of_u, row_idx, inside_row = _nbr_windows(rows)
    qc = jnp.arange(W)[:, None]
    kc = jnp.arange(W)[None, :]
    c_start = jnp.clip(qc - kw // 2, 0, W - kw)
    inside_col = (kc >= c_start) & (kc < c_start + kw)
    col_idx = jnp.clip(kc - qc + (kw - 1), 0, 2 * kw - 2)
    row_sel = jax.nn.one_hot(jnp.asarray(row_idx, jnp.int32), 2 * C_WIN_H - 1, dtype=F32)
    col_sel = jax.nn.one_hot(col_idx, 2 * kw - 1, dtype=F32)
    b = jnp.einsum('hrc,vair->hvaic', rpb, row_sel, precision=lax.Precision.HIGHEST)
    b = jnp.einsum('hvaic,qkc->hvaiqk', b, col_sel, precision=lax.Precision.HIGHEST)
    ok = jnp.asarray(inside_row)[None, :, :, :, None, None] & inside_col[None, None, None, None]
    b = jnp.where(ok, b, NEG).transpose(1, 0, 3, 5, 2, 4)
    nv = len(row_idx)
    return b.reshape(nv, C_HEADS, 2 * NBR_KBLK * W, NBR_Q).astype(F32), jnp.asarray(var_of_u, jnp.int32)


def _nbr_attention(q, k, vt, bias, var_of_u):
    B, H, S, d = q.shape
    nu = S // NBR_Q

    def first(u):
        return jnp.clip(u - C_WIN_H // 4, 0, nu - NBR_KBLK)

    k_specs = [pl.BlockSpec((1, H, NBR_Q, d), functools.partial(lambda b, u, var, i: (b, 0, first(u) + i, 0), i=i))
               for i in range(NBR_KBLK)]
    vt_specs = [pl.BlockSpec((1, H, 1, V_ROWS, NBR_Q), functools.partial(lambda b, u, var, i: (b, 0, first(u) + i, 0, 0), i=i))
                for i in range(NBR_KBLK)]
    grid_spec = pltpu.PrefetchScalarGridSpec(
        num_scalar_prefetch=1,
        grid=(B, nu),
        in_specs=[pl.BlockSpec((1, H, NBR_Q, d), lambda b, u, var: (b, 0, u, 0))] + k_specs + vt_specs
        + [pl.BlockSpec((1, H, NBR_KBLK * NBR_Q, NBR_Q), lambda b, u, var: (var[u], 0, 0, 0))],
        out_specs=pl.BlockSpec((1, NBR_Q, H * d), lambda b, u, var: (b, u, 0)),
    )
    return pl.pallas_call(
        _nbr_kernel,
        grid_spec=grid_spec,
        out_shape=jax.ShapeDtypeStruct((B, S, H * d), BF16),
        compiler_params=_params("parallel", "arbitrary"),
        name="nbr_attention",
    )(var_of_u, q, *([k] * NBR_KBLK), *([vt] * NBR_KBLK), bias)


def _out_ln_kernel(a_ref, w_ref, x_ref, g_ref, b_ref, o_ref):
    h = jnp.dot(a_ref[...], w_ref[...], preferred_element_type=F32)
    o_ref[...] = _layer_norm(DEEPNORM_ALPHA * x_ref[...] + h, g_ref[...], b_ref[...])


def _out_ln(a, w, x, g, b, bm):
    N = x.shape[0]
    row = lambda i: (i, 0)
    return pl.pallas_call(
        _out_ln_kernel,
        grid=(N // bm,),
        in_specs=[pl.BlockSpec((bm, a.shape[1]), row), _full(w.shape), pl.BlockSpec((bm, D_MODEL), row),
                  _full(g.shape), _full(b.shape)],
        out_specs=pl.BlockSpec((bm, D_MODEL), row),
        out_shape=jax.ShapeDtypeStruct((N, D_MODEL), F32),
        compiler_params=_params("parallel"),
        name="out_proj_ln",
    )(a, w, x, g, b)


def _swiglu_up_kernel(x_ref, wg_ref, wu_ref, h_ref):
    xb = x_ref[...].astype(BF16)
    g = jnp.dot(xb, wg_ref[...], preferred_element_type=F32)
    u = jnp.dot(xb, wu_ref[...], preferred_element_type=F32)
    h_ref[...] = (g * jax.nn.sigmoid(g) * u).astype(BF16)


def _swiglu_up(x, wg, wu, bm, bn):
    N = x.shape[0]
    F = wg.shape[1]
    return pl.pallas_call(
        _swiglu_up_kernel,
        grid=(F // bn, N // bm),
        in_specs=[pl.BlockSpec((bm, D_MODEL), lambda j, i: (i, 0)), pl.BlockSpec((D_MODEL, bn), lambda j, i: (0, j)),
                  pl.BlockSpec((D_MODEL, bn), lambda j, i: (0, j))],
        out_specs=pl.BlockSpec((bm, bn), lambda j, i: (i, j)),
        out_shape=jax.ShapeDtypeStruct((N, F), BF16),
        compiler_params=_params("parallel", "parallel"),
        name="swiglu_up",
    )(x, wg, wu)


def _ple(x, p_ref, wpg_ref, wpi_ref):
    gate = jax.nn.sigmoid(jnp.dot(x.astype(BF16), wpg_ref[...], preferred_element_type=F32))
    return gate * jnp.dot(p_ref[0].astype(BF16), wpi_ref[...], preferred_element_type=F32)


def _ffn_down_ln_kernel(h_ref, wd_ref, x_ref, p_ref, wpg_ref, wpi_ref, g_ref, b_ref, o_ref):
    x = x_ref[...]
    f = jnp.dot(h_ref[...], wd_ref[...], preferred_element_type=F32)
    o_ref[...] = _layer_norm(DEEPNORM_ALPHA * x + f + _ple(x, p_ref, wpg_ref, wpi_ref), g_ref[...], b_ref[...])


def _ffn_down_ln(h, wd, x, p, layer, wpg, wpi, g, b, bm):
    N = x.shape[0]
    row = lambda i: (i, 0)
    return pl.pallas_call(
        _ffn_down_ln_kernel,
        grid=(N // bm,),
        in_specs=[pl.BlockSpec((bm, h.shape[1]), row), _full(wd.shape), pl.BlockSpec((bm, D_MODEL), row),
                  pl.BlockSpec((1, bm, PLE_DIM), lambda i: (layer, i, 0)), _full(wpg.shape), _full(wpi.shape),
                  _full(g.shape), _full(b.shape)],
        out_specs=pl.BlockSpec((bm, D_MODEL), row),
        out_shape=jax.ShapeDtypeStruct((N, D_MODEL), F32),
        compiler_params=_params("parallel"),
        name="ffn_down_ple_ln",
    )(h, wd, x, p, wpg, wpi, g, b)


def _router_kernel(x_ref, w_ref, idx_ref, gate_ref, xb_ref):
    x = x_ref[...]
    xb_ref[...] = x.astype(BF16)
    logits = jnp.dot(x, w_ref[...], preferred_element_type=F32, precision=lax.Precision.HIGHEST)
    lane = lax.broadcasted_iota(jnp.int32, logits.shape, 1).astype(F32)
    logits = jnp.where(lane < N_EXPERTS, logits, NEG)
    m1 = jnp.max(logits, axis=-1, keepdims=True)
    i1 = jnp.min(jnp.where(logits == m1, lane, 128.0), axis=-1, keepdims=True)
    rest = jnp.where(lane == i1, NEG, logits)
    m2 = jnp.max(rest, axis=-1, keepdims=True)
    i2 = jnp.min(jnp.where(rest == m2, lane, 128.0), axis=-1, keepdims=True)
    e = jnp.exp(m2 - m1)
    g1 = 1.0 / (1.0 + e)
    idx_ref[...] = jnp.where(lane == 0.0, i1, jnp.where(lane == 1.0, i2, 0.0)).astype(jnp.int32)
    gate_ref[...] = jnp.where(lane == 0.0, g1, jnp.where(lane == 1.0, e * g1, 0.0))


def _router(x, w, bm):
    N = x.shape[0]
    row = lambda i: (i, 0)
    return pl.pallas_call(
        _router_kernel,
        grid=(N // bm,),
        in_specs=[pl.BlockSpec((bm, D_MODEL), row), _full(w.shape)],
        out_specs=[pl.BlockSpec((bm, 128), row)] * 2 + [pl.BlockSpec((bm, D_MODEL), row)],
        out_shape=[jax.ShapeDtypeStruct((N, 128), jnp.int32), jax.ShapeDtypeStruct((N, 128), F32),
                   jax.ShapeDtypeStruct((N, D_MODEL), BF16)],
        compiler_params=_params("parallel"),
        name="moe_router",
    )(x, w)


def _moe_ffn_kernel(te_ref, nt_ref, xs_ref, wg_ref, wu_ref, wd_ref, o_ref, acc_ref):
    t = pl.program_id(0)
    j = pl.program_id(1)

    @pl.when(t < nt_ref[0])
    def _():
        xs = xs_ref[...]
        g = jnp.dot(xs, wg_ref[0, 0], preferred_element_type=F32)
        u = jnp.dot(xs, wu_ref[0, 0], preferred_element_type=F32)
        h = (g * jax.nn.sigmoid(g) * u).astype(BF16)
        y = jnp.dot(h, wd_ref[0, 0], preferred_element_type=F32)

        @pl.when(j == 0)
        def _():
            acc_ref[...] = y

        @pl.when(j == pl.num_programs(1) - 1)
        def _():
            o_ref[...] = (acc_ref[...] + y).astype(o_ref.dtype)

    @pl.when(t >= nt_ref[0])
    def _():
        o_ref[...] = jnp.zeros_like(o_ref)


def _moe_ffn(tile_expert, n_tiles, xs, wg, wu, wd, layer, tm, halves=2):
    P = xs.shape[0]
    fh = FF_EXPERT // halves
    grid_spec = pltpu.PrefetchScalarGridSpec(
        num_scalar_prefetch=2,
        grid=(P // tm, halves),
        in_specs=[pl.BlockSpec((tm, D_MODEL), lambda t, j, te, nt: (t, 0)),
                  pl.BlockSpec((1, 1, D_MODEL, fh), lambda t, j, te, nt: (layer, te[t], 0, j)),
                  pl.BlockSpec((1, 1, D_MODEL, fh), lambda t, j, te, nt: (layer, te[t], 0, j)),
                  pl.BlockSpec((1, 1, fh, D_MODEL), lambda t, j, te, nt: (layer, te[t], j, 0))],
        out_specs=pl.BlockSpec((tm, D_MODEL), lambda t, j, te, nt: (t, 0)),
        scratch_shapes=[pltpu.VMEM((tm, D_MODEL), F32)],
    )
    return pl.pallas_call(
        _moe_ffn_kernel,
        grid_spec=grid_spec,
        out_shape=jax.ShapeDtypeStruct((P, D_MODEL), BF16),
        compiler_params=_params("arbitrary", "arbitrary"),
        name="moe_expert_ffn",
    )(tile_expert, n_tiles, xs, wg, wu, wd)


def _moe_combine_ln_kernel(ya_ref, yb_ref, gate_ref, x_ref, p_ref, wpg_ref, wpi_ref, g_ref, b_ref, o_ref):
    x = x_ref[...]
    gate = gate_ref[...]
    f = gate[:, 0:1] * ya_ref[...] + gate[:, 1:2] * yb_ref[...]
    o_ref[...] = _layer_norm(DEEPNORM_ALPHA * x + f + _ple(x, p_ref, wpg_ref, wpi_ref), g_ref[...], b_ref[...])


def _moe_combine_ln(ya, yb, gates, x, p, layer, wpg, wpi, g, b, bm):
    N = x.shape[0]
    row = lambda i: (i, 0)
    rows = pl.BlockSpec((bm, D_MODEL), row)
    return pl.pallas_call(
        _moe_combine_ln_kernel,
        grid=(N // bm,),
        in_specs=[rows, rows, pl.BlockSpec((bm, 128), row), rows, pl.BlockSpec((1, bm, PLE_DIM), lambda i: (layer, i, 0)),
                  _full(wpg.shape), _full(wpi.shape), _full(g.shape), _full(b.shape)],
        out_specs=rows,
        out_shape=jax.ShapeDtypeStruct((N, D_MODEL), F32),
        compiler_params=_params("parallel"),
        name="moe_combine_ple_ln",
    )(ya, yb, gates, x, p, wpg, wpi, g, b)


def _route(idx, tm):
    N = idx.shape[0]
    e_flat = idx.reshape(-1)
    onehot = (e_flat[:, None] == jnp.arange(N_EXPERTS, dtype=jnp.int32)[None, :]).astype(jnp.int32)
    csum = jnp.cumsum(onehot, axis=0)
    rank = jnp.sum(csum * onehot, axis=1) - 1
    counts = csum[-1]
    padded = ((counts + tm - 1) // tm) * tm
    ends = jnp.cumsum(padded)
    slot = (ends - padded)[e_flat] + rank
    P = TOP_K * N + N_EXPERTS * tm
    row_token = jnp.zeros((P,), jnp.int32).at[slot].set(
        jnp.arange(TOP_K * N, dtype=jnp.int32) // TOP_K, unique_indices=True, mode="promise_in_bounds")
    tile_start = jnp.arange(P // tm, dtype=jnp.int32) * tm
    tile_expert = jnp.minimum(jnp.sum((tile_start[:, None] >= ends[None, :]).astype(jnp.int32), axis=1), N_EXPERTS - 1)
    n_tiles = (ends[-1] // tm).astype(jnp.int32).reshape(1)
    return slot.reshape(N, TOP_K), row_token, tile_expert.astype(jnp.int32), n_tiles


def _swap_halves(w, group):
    shp = w.shape
    w = w.reshape(shp[:-1] + (shp[-1] // group, 2, group // 2))
    return jnp.flip(w, axis=-2).reshape(shp)


def _rope_tables(pos, dim):
    inv = ROPE_THETA ** (-jnp.arange(0, dim, 2, dtype=F32) / dim)
    ang = pos.astype(F32)[:, None] * inv[None, :]
    c, s = jnp.cos(ang), jnp.sin(ang)
    return jnp.concatenate([c, c], -1), jnp.concatenate([-s, s], -1)


def _mla_prep(w_dq, w_dkv, w_uq, w_ukv, S):
    wd = jnp.zeros((D_MODEL, 896), F32)
    wd = wd.at[:, :A_Q_LORA].set(w_dq).at[:, A_Q_LORA:640].set(w_dkv[:, :A_KV_LORA])
    w_r = w_dkv[:, A_KV_LORA:]
    wd = wd.at[:, 640 + A_NOPE:640 + A_NOPE + A_ROPE].set(w_r)
    wd = wd.at[:, 768 + A_NOPE:768 + A_NOPE + A_ROPE].set(_swap_halves(w_r, A_ROPE))
    uq = w_uq.reshape(A_Q_LORA, A_HEADS, A_NOPE + A_ROPE)
    w1 = jnp.pad(uq, ((0, 0), (0, 0), (0, A_DK - A_NOPE - A_ROPE))).reshape(A_Q_LORA, A_HEADS * A_DK)
    w2 = jnp.pad(_swap_halves(uq[..., A_NOPE:], A_ROPE), ((0, 0), (0, 0), (A_NOPE, A_DK - A_NOPE - A_ROPE)))
    w2 = w2.reshape(A_Q_LORA, A_HEADS * A_DK)
    ukv = w_ukv.reshape(A_KV_LORA, A_HEADS, A_NOPE + A_V)
    wk = jnp.pad(ukv[..., :A_NOPE], ((0, 0), (0, 0), (0, A_DK - A_NOPE))).reshape(A_KV_LORA, A_HEADS * A_DK)
    wvt = ukv[..., A_NOPE:].reshape(A_KV_LORA, A_HEADS * A_V).T
    c, s = _rope_tables(jnp.arange(S), A_ROPE)
    pad = lambda t, fill: jnp.concatenate([jnp.full((S, A_NOPE), fill, F32), t, jnp.zeros((S, A_DK - A_NOPE - A_ROPE), F32)], -1)
    scale = (A_NOPE + A_ROPE) ** -0.5 * LOG2E
    tabs = dict(k_cos=pad(c, 0.0), k_sin=pad(s, 0.0), q_cos=pad(c, 1.0) * scale, q_sin=pad(s, 0.0) * scale)
    return wd.astype(BF16), w1.astype(BF16), w2.astype(BF16), wk.astype(BF16), wvt.astype(BF16), tabs


def _gqa_prep(w_qkv, g_q, g_k, S):
    nq = B_Q_HEADS * B_HEAD_DIM
    nk = B_KV_HEADS * B_HEAD_DIM
    half = B_HEAD_DIM // 2
    wq, wk, wv = w_qkv[:, :nq], w_qkv[:, nq:nq + nk], w_qkv[:, nq + nk:]
    wkk = jnp.concatenate([wk, _swap_halves(wk, half)], axis=1)
    t = jnp.arange(S)
    cr, sr = _rope_tables(t // GRID_W, half)
    cc, sc = _rope_tables(t % GRID_W, half)
    cos_t = jnp.tile(jnp.concatenate([cr, cc], -1), (1, 2))
    sin_t = jnp.tile(jnp.concatenate([sr, sc], -1), (1, 2))
    lane = jnp.arange(nq) // B_HEAD_DIM
    ones_bd = (lane[:, None] == lane[None, :]).astype(BF16)
    gq = jnp.tile(g_q, B_Q_HEADS)[None, :]
    gk = jnp.tile(g_k, B_KV_HEADS)[None, :]
    return (wq.astype(BF16), _swap_halves(wq, half).astype(BF16), wkk.astype(BF16), wv.T.astype(BF16), ones_bd,
            gq, _swap_halves(gq, half), gk, _swap_halves(gk, half), cos_t, sin_t)


def _trunk(x, p, w):
    B, S, _ = x.shape
    N = B * S
    bm = min(ROW_BLOCK, S)
    tq = min(512, S)
    x = x.reshape(N, D_MODEL)
    p = p.reshape(p.shape[0], N, PLE_DIM)
    for i in range(DEPTH):
        kind, j = i % N_MIXERS, i // N_MIXERS
        if kind == 0:
            wd, w1, w2, wk, wvt, tabs = _mla_prep(w['a_w_dq'][j], w['a_w_dkv'][j], w['a_w_uq'][j], w['a_w_ukv'][j], S)
            cq, ckv, kr = _mla_down(x, wd, w['a_g_q'][j][None, :], w['a_g_kv'][j][None, :], tabs['k_cos'], tabs['k_sin'], S, bm)
            q = _mla_q_up(cq, w1, w2, tabs['q_cos'], tabs['q_sin'], B, S, bm)
            k, vt = _mla_kv_up(ckv, kr, wk, wvt, B, S, min(KV_CHUNK, S))
            a = _flash(q, k, vt, hp=2, G=1, tq=tq, nbuf=4)
            w_o = w['a_w_o'][j]
        elif kind == 1:
            prep = _gqa_prep(w['b_w_qkv'][j], w['b_g_q'][j], w['b_g_k'][j], S)
            q, k, vt = _gqa_proj(x, *prep, B, S, bm)
            a = _flash(q, k, vt, hp=1, G=B_Q_HEADS // B_KV_HEADS, tq=min(256, S), nbuf=2)
            w_o = w['b_w_o'][j]
        else:
            n = C_HEADS * C_HEAD_DIM
            w_qkv = w['c_w_qkv'][j]
            q, k, vt = _nbr_proj(x, w_qkv[:, :2 * n].astype(BF16), w_qkv[:, 2 * n:].T.astype(BF16), B, S, bm)
            a = _nbr_attention(q, k, vt, *_nbr_bias(w['c_rpb'][j], S // GRID_W))
            w_o = w['c_w_o'][j]
        x = _out_ln(a.reshape(N, D_MODEL), w_o.astype(BF16), x, w['ln1_g'][i][None, :], w['ln1_b'][i][None, :], bm)
        f_i = i // 2
        wpg, wpi = w['ple_w_gate'][i].astype(BF16), w['ple_w_in'][i].astype(BF16)
        g2, b2 = w['ln2_g'][i][None, :], w['ln2_b'][i][None, :]
        if i % 2 == 0:
            h = _swiglu_up(x, w['f_w_gate'][f_i].astype(BF16), w['f_w_up'][f_i].astype(BF16), bm, FF_DENSE // 2)
            x = _ffn_down_ln(h, w['f_w_down'][f_i].astype(BF16), x, p, i, wpg, wpi, g2, b2, bm)
        else:
            w_r = jnp.pad(w['m_w_router'][f_i], ((0, 0), (0, 128 - N_EXPERTS)))
            idx, gates, xb = _router(x, w_r, bm)
            slot, row_token, tile_expert, n_tiles = _route(idx[:, :TOP_K], MOE_TILE)
            xs = xb.at[row_token].get(mode="promise_in_bounds")
            ys = _moe_ffn(tile_expert, n_tiles, xs, w['m_w_gate'].astype(BF16), w['m_w_up'].astype(BF16),
                          w['m_w_down'].astype(BF16), f_i, MOE_TILE)
            ya = ys.at[slot[:, 0]].get(mode="promise_in_bounds")
            yb = ys.at[slot[:, 1]].get(mode="promise_in_bounds")
            x = _moe_combine_ln(ya, yb, gates, x, p, i, wpg, wpi, g2, b2, bm)
    return x.reshape(B, S, D_MODEL)


def kernel(x_prompt, x_sample, p_prompt, p_sample, a_w_dq, a_g_q, a_w_uq, a_w_dkv, a_g_kv, a_w_ukv, a_w_o, b_w_qkv, b_g_q, b_g_k, b_w_o, c_w_qkv, c_rpb, c_w_o, ln1_g, ln1_b, ln2_g, ln2_b, f_w_gate, f_w_up, f_w_down, m_w_router, m_w_gate, m_w_up, m_w_down, ple_w_gate, ple_w_in):
    w = dict(a_w_dq=a_w_dq, a_g_q=a_g_q, a_w_uq=a_w_uq, a_w_dkv=a_w_dkv, a_g_kv=a_g_kv, a_w_ukv=a_w_ukv, a_w_o=a_w_o,
             b_w_qkv=b_w_qkv, b_g_q=b_g_q, b_g_k=b_g_k, b_w_o=b_w_o, c_w_qkv=c_w_qkv, c_rpb=c_rpb, c_w_o=c_w_o,
             ln1_g=ln1_g, ln1_b=ln1_b, ln2_g=ln2_g, ln2_b=ln2_b, f_w_gate=f_w_gate, f_w_up=f_w_up, f_w_down=f_w_down,
             m_w_router=m_w_router, m_w_gate=m_w_gate, m_w_up=m_w_up, m_w_down=m_w_down,
             ple_w_gate=ple_w_gate, ple_w_in=ple_w_in)
    return (_trunk(x_prompt, p_prompt, w), _trunk(x_sample, p_sample, w))
```

```python
import functools

import jax
import jax.numpy as jnp
from jax import lax
from jax.experimental import pallas as pl
from jax.experimental.pallas import tpu as pltpu

F32 = jnp.float32
BF16 = jnp.bfloat16

D_MODEL = 1024
DEPTH = 4
GRID_W = 64
PLE_DIM = 256
N_MIXERS = 3
ROPE_THETA = 10000.0
RMS_EPS = 1e-6
LN_EPS = 1e-5
DEEPNORM_ALPHA = (2 * DEPTH) ** 0.25

A_HEADS = 16
A_Q_LORA = 384
A_KV_LORA = 256
A_NOPE = 64
A_ROPE = 32
A_V = 64
A_DK = 128
B_Q_HEADS = 16
B_KV_HEADS = 4
B_HEAD_DIM = 64
C_HEADS = 16
C_HEAD_DIM = 64
C_WIN_H = 8
C_WIN_W = 16
FF_DENSE = 2816
N_EXPERTS = 8
TOP_K = 2
FF_EXPERT = 3584

V_ROWS = 80
NEG = -1e30
LOG2E = 1.4426950408889634
VMEM_LIMIT = 56 * 1024 * 1024
ROW_BLOCK = 512
KV_CHUNK = 512
MOE_TILE = 512
NBR_Q = 2 * GRID_W
NBR_KBLK = (C_WIN_H + 2) // 2
NT = (((1,), (1,)), ((), ()))


def _params(*sem):
    return pltpu.CompilerParams(dimension_semantics=sem, vmem_limit_bytes=VMEM_LIMIT)


def _full(shape):
    return pl.BlockSpec(shape, lambda *_: (0,) * len(shape))


def _rms(x, g):
    return x * lax.rsqrt(jnp.mean(x * x, axis=-1, keepdims=True) + RMS_EPS) * g


def _layer_norm(x, g, b):
    mu = jnp.mean(x, axis=-1, keepdims=True)
    xc = x - mu
    var = jnp.mean(xc * xc, axis=-1, keepdims=True)
    return xc * lax.rsqrt(var + LN_EPS) * g + b


def _tile_lanes(t, n):
    return jnp.concatenate([t] * n, axis=-1)


def _mla_down_kernel(x_ref, w_ref, gq_ref, gkv_ref, cos_ref, sin_ref, cq_ref, ckv_ref, kr_ref):
    acc = jnp.dot(x_ref[...].astype(BF16), w_ref[...], preferred_element_type=F32)
    cq_ref[...] = _rms(acc[:, :A_Q_LORA], gq_ref[...]).astype(BF16)
    ckv_ref[...] = _rms(acc[:, A_Q_LORA:640], gkv_ref[...]).astype(BF16)
    kr_ref[...] = (acc[:, 640:768] * cos_ref[...] + acc[:, 768:896] * sin_ref[...]).astype(BF16)


def _mla_down(x, w, gq, gkv, cos_t, sin_t, S, bm):
    N = x.shape[0]
    nsb = S // bm
    row = lambda i: (i, 0)
    tab = lambda i: (i % nsb, 0)
    return pl.pallas_call(
        _mla_down_kernel,
        grid=(N // bm,),
        in_specs=[pl.BlockSpec((bm, D_MODEL), row), _full(w.shape), _full(gq.shape), _full(gkv.shape),
                  pl.BlockSpec((bm, 128), tab), pl.BlockSpec((bm, 128), tab)],
        out_specs=[pl.BlockSpec((bm, A_Q_LORA), row), pl.BlockSpec((bm, A_KV_LORA), row),
                   pl.BlockSpec((bm, 128), row)],
        out_shape=[jax.ShapeDtypeStruct((N, A_Q_LORA), BF16), jax.ShapeDtypeStruct((N, A_KV_LORA), BF16),
                   jax.ShapeDtypeStruct((N, 128), BF16)],
        compiler_params=_params("parallel"),
        name="mla_down",
    )(x, w, gq, gkv, cos_t, sin_t)


def _mla_q_up_kernel(cq_ref, w1_ref, w2_ref, cos_ref, sin_ref, q_ref):
    cq = cq_ref[...]
    a1 = jnp.dot(cq, w1_ref[...], preferred_element_type=F32)
    a2 = jnp.dot(cq, w2_ref[...], preferred_element_type=F32)
    q = a1 * _tile_lanes(cos_ref[...], A_HEADS) + a2 * _tile_lanes(sin_ref[...], A_HEADS)
    for h in range(A_HEADS):
        q_ref[0, h] = q[:, h * A_DK:(h + 1) * A_DK].astype(BF16)


def _mla_q_up(cq, w1, w2, cos_t, sin_t, B, S, bm):
    nsb = S // bm
    return pl.pallas_call(
        _mla_q_up_kernel,
        grid=(B * nsb,),
        in_specs=[pl.BlockSpec((bm, A_Q_LORA), lambda i: (i, 0)), _full(w1.shape), _full(w2.shape),
                  pl.BlockSpec((bm, 128), lambda i: (i % nsb, 0)), pl.BlockSpec((bm, 128), lambda i: (i % nsb, 0))],
        out_specs=pl.BlockSpec((1, A_HEADS, bm, A_DK), lambda i: (i // nsb, 0, i % nsb, 0)),
        out_shape=jax.ShapeDtypeStruct((B, A_HEADS, S, A_DK), BF16),
        compiler_params=_params("parallel"),
        name="mla_q_up",
    )(cq, w1, w2, cos_t, sin_t)


def _ones_rows(n):
    r = lax.broadcasted_iota(jnp.int32, (V_ROWS - 64, n), 0)
    return jnp.where(r == 0, 1.0, 0.0).astype(BF16)


def _mla_kv_up_kernel(ckv_ref, kr_ref, wk_ref, wvt_ref, k_ref, vt_ref):
    ckv = ckv_ref[...]
    ak = jnp.dot(ckv, wk_ref[...], preferred_element_type=F32)
    kr = kr_ref[...].astype(F32)
    avt = lax.dot_general(wvt_ref[...], ckv, NT, preferred_element_type=F32)
    ones = _ones_rows(ckv.shape[0])
    for h in range(A_HEADS):
        k_ref[0, h] = (ak[:, h * A_DK:(h + 1) * A_DK] + kr).astype(BF16)
        vt_ref[0, h, 0, 0:A_V, :] = avt[h * A_V:(h + 1) * A_V, :].astype(BF16)
        vt_ref[0, h, 0, A_V:V_ROWS, :] = ones


def _mla_kv_up(ckv, kr, wk, wvt, B, S, bm):
    nsb = S // bm
    return pl.pallas_call(
        _mla_kv_up_kernel,
        grid=(B * nsb,),
        in_specs=[pl.BlockSpec((bm, A_KV_LORA), lambda i: (i, 0)), pl.BlockSpec((bm, 128), lambda i: (i, 0)),
                  _full(wk.shape), _full(wvt.shape)],
        out_specs=[pl.BlockSpec((1, A_HEADS, bm, A_DK), lambda i: (i // nsb, 0, i % nsb, 0)),
                   pl.BlockSpec((1, A_HEADS, 1, V_ROWS, bm), lambda i: (i // nsb, 0, i % nsb, 0, 0))],
        out_shape=[jax.ShapeDtypeStruct((B, A_HEADS, S, A_DK), BF16),
                   jax.ShapeDtypeStruct((B, A_HEADS, nsb, V_ROWS, bm), BF16)],
        compiler_params=_params("parallel"),
        name="mla_kv_up",
    )(ckv, kr, wk, wvt)


def _gqa_proj_kernel(x_ref, wq_ref, wqp_ref, wk_ref, wvt_ref, ones_ref, gq_ref, gqp_ref, gk_ref, gkp_ref,
                     cos_ref, sin_ref, q_ref, k_ref, vt_ref):
    xb = x_ref[...].astype(BF16)
    nq = B_Q_HEADS * B_HEAD_DIM
    nk = B_KV_HEADS * B_HEAD_DIM
    cos = cos_ref[...]
    sin = sin_ref[...]
    ones_bd = ones_ref[...]

    def norm_rope(a, ap, g, gp, n):
        ss = jnp.dot((a * a).astype(BF16), ones_bd[:n, :n], preferred_element_type=F32) * (1.0 / B_HEAD_DIM)
        r = lax.rsqrt(ss + RMS_EPS)
        reps = n // 128
        return r * (a * g * _tile_lanes(cos, reps) + ap * gp * _tile_lanes(sin, reps))

    aq = jnp.dot(xb, wq_ref[...], preferred_element_type=F32)
    aqp = jnp.dot(xb, wqp_ref[...], preferred_element_type=F32)
    q = norm_rope(aq, aqp, gq_ref[...], gqp_ref[...], nq) * (B_HEAD_DIM ** -0.5 * LOG2E)
    akk = jnp.dot(xb, wk_ref[...], preferred_element_type=F32)
    k = norm_rope(akk[:, :nk], akk[:, nk:], gk_ref[...], gkp_ref[...], nk)
    avt = lax.dot_general(wvt_ref[...], xb, NT, preferred_element_type=F32)
    ones = _ones_rows(xb.shape[0])
    for h in range(B_Q_HEADS):
        q_ref[0, h] = q[:, h * B_HEAD_DIM:(h + 1) * B_HEAD_DIM].astype(BF16)
    for h in range(B_KV_HEADS):
        k_ref[0, h] = k[:, h * B_HEAD_DIM:(h + 1) * B_HEAD_DIM].astype(BF16)
        vt_ref[0, h, 0, 0:B_HEAD_DIM, :] = avt[h * B_HEAD_DIM:(h + 1) * B_HEAD_DIM, :].astype(BF16)
        vt_ref[0, h, 0, B_HEAD_DIM:V_ROWS, :] = ones


def _gqa_proj(x, wq, wqp, wk, wvt, ones_bd, gq, gqp, gk, gkp, cos_t, sin_t, B, S, bm):
    nsb = S // bm
    hm = lambda i: (i // nsb, 0, i % nsb, 0)
    tab = lambda i: (i % nsb, 0)
    ins = [x, wq, wqp, wk, wvt, ones_bd, gq, gqp, gk, gkp, cos_t, sin_t]
    specs = [pl.BlockSpec((bm, D_MODEL), lambda i: (i, 0))] + [_full(a.shape) for a in ins[1:10]]
    specs += [pl.BlockSpec((bm, 128), tab), pl.BlockSpec((bm, 128), tab)]
    return pl.pallas_call(
        _gqa_proj_kernel,
        grid=(B * nsb,),
        in_specs=specs,
        out_specs=[pl.BlockSpec((1, B_Q_HEADS, bm, B_HEAD_DIM), hm), pl.BlockSpec((1, B_KV_HEADS, bm, B_HEAD_DIM), hm),
                   pl.BlockSpec((1, B_KV_HEADS, 1, V_ROWS, bm), lambda i: (i // nsb, 0, i % nsb, 0, 0))],
        out_shape=[jax.ShapeDtypeStruct((B, B_Q_HEADS, S, B_HEAD_DIM), BF16),
                   jax.ShapeDtypeStruct((B, B_KV_HEADS, S, B_HEAD_DIM), BF16),
                   jax.ShapeDtypeStruct((B, B_KV_HEADS, nsb, V_ROWS, bm), BF16)],
        compiler_params=_params("parallel"),
        name="gqa_proj",
    )(*ins)


def _flash_kernel(q_ref, k_ref, vt_ref, o_ref, *scratch, hp, G, tq, tk, nk, dv, unroll, nbuf):
    s_bufs, acc_ref = scratch[:nbuf], scratch[nbuf]
    TQ = G * tq
    dk = q_ref.shape[-1]
    outs = []
    for h in range(hp):
        q = q_ref[0, h * G:(h + 1) * G].reshape(TQ, dk)

        def scores(c, s_ref, h=h, q=q):
            off = c * tk if isinstance(c, int) else pl.multiple_of(c * tk, tk)
            s = lax.dot_general(k_ref[0, h, pl.ds(off, tk), :], q, NT, preferred_element_type=F32)
            s_ref[...] = s
            return jnp.max(s, axis=0, keepdims=True)

        def step(c, i, m, a_cur, last, h=h):
            if not last:
                cm = scores(c + 1, s_bufs[(i + 1) % nbuf])
            p = jnp.exp2((s_bufs[i % nbuf][...] - m).astype(BF16))
            acc_ref[...] = a_cur * acc_ref[...] + jnp.dot(vt_ref[0, h, c], p, preferred_element_type=F32)
            if last:
                return m, a_cur
            m_new = jnp.maximum(m, cm)
            return m_new, jnp.exp2(m - m_new)

        def group(jj, carry, last=False):
            m, a_cur = carry
            for i in range(unroll):
                m, a_cur = step(unroll * jj + i, i, m, a_cur, last and i == unroll - 1)
            return m, a_cur

        acc_ref[...] = jnp.zeros_like(acc_ref)
        m0 = jnp.maximum(scores(0, s_bufs[0]), NEG)
        ng = nk // unroll
        carry = (m0, jnp.ones((1, TQ), F32))
        if ng > 1:
            carry = lax.fori_loop(0, ng - 1, group, carry)
        group(ng - 1, carry, last=True)
        acc = acc_ref[...]
        o = acc[:dv] * (1.0 / acc[dv:dv + 1])
        for g in range(G):
            outs.append(o[:, g * tq:(g + 1) * tq])
    o_ref[0] = jnp.concatenate(outs, axis=0).T.astype(o_ref.dtype)


def _flash(q, k, vt, *, hp, G, tq, nbuf, dv=64):
    B, H, S, dk = q.shape
    Hk = k.shape[1]
    nk, tk = vt.shape[2], vt.shape[4]
    unroll = 8 if nk % 8 == 0 and nk > 8 else 4 if nk % 4 == 0 else 2
    nbuf = min(nbuf, unroll)
    assert nk % unroll == 0 and unroll % nbuf == 0
    TQ = G * tq
    scratch = [pltpu.VMEM((tk, TQ), F32)] * nbuf + [pltpu.VMEM((V_ROWS, TQ), F32)]
    kern = functools.partial(_flash_kernel, hp=hp, G=G, tq=tq, tk=tk, nk=nk, dv=dv, unroll=unroll, nbuf=nbuf)
    return pl.pallas_call(
        kern,
        grid=(B, Hk // hp, S // tq),
        in_specs=[pl.BlockSpec((1, hp * G, tq, dk), lambda b, h, i: (b, h, i, 0)),
                  pl.BlockSpec((1, hp, S, dk), lambda b, h, i: (b, h, 0, 0)),
                  pl.BlockSpec((1, hp, nk, V_ROWS, tk), lambda b, h, i: (b, h, 0, 0, 0))],
        out_specs=pl.BlockSpec((1, tq, hp * G * dv), lambda b, h, i: (b, i, h)),
        out_shape=jax.ShapeDtypeStruct((B, S, H * dv), BF16),
        scratch_shapes=scratch,
        compiler_params=_params("parallel", "parallel", "arbitrary"),
        name="flash_attention",
    )(q, k, vt)


def _nbr_proj_kernel(x_ref, w_ref, wvt_ref, q_ref, k_ref, vt_ref):
    xb = x_ref[...].astype(BF16)
    n = C_HEADS * C_HEAD_DIM
    acc = jnp.dot(xb, w_ref[...], preferred_element_type=F32)
    avt = lax.dot_general(wvt_ref[...], xb, NT, preferred_element_type=F32)
    ones = _ones_rows(NBR_Q)
    for h in range(C_HEADS):
        lo = h * C_HEAD_DIM
        q_ref[0, h] = (acc[:, lo:lo + C_HEAD_DIM] * (C_HEAD_DIM ** -0.5)).astype(BF16)
        k_ref[0, h] = acc[:, n + lo:n + lo + C_HEAD_DIM].astype(BF16)
        for c in range(xb.shape[0] // NBR_Q):
            vt_ref[0, h, c, 0:C_HEAD_DIM, :] = avt[lo:lo + C_HEAD_DIM, c * NBR_Q:(c + 1) * NBR_Q].astype(BF16)
            vt_ref[0, h, c, C_HEAD_DIM:V_ROWS, :] = ones


def _nbr_proj(x, w, wvt, B, S, bm):
    nsb = S // bm
    hm = lambda i: (i // nsb, 0, i % nsb, 0)
    shp = jax.ShapeDtypeStruct((B, C_HEADS, S, C_HEAD_DIM), BF16)
    return pl.pallas_call(
        _nbr_proj_kernel,
        grid=(B * nsb,),
        in_specs=[pl.BlockSpec((bm, D_MODEL), lambda i: (i, 0)), _full(w.shape), _full(wvt.shape)],
        out_specs=[pl.BlockSpec((1, C_HEADS, bm, C_HEAD_DIM), hm)] * 2
        + [pl.BlockSpec((1, C_HEADS, bm // NBR_Q, V_ROWS, NBR_Q), lambda i: (i // nsb, 0, i % nsb, 0, 0))],
        out_shape=[shp, shp, jax.ShapeDtypeStruct((B, C_HEADS, S // NBR_Q, V_ROWS, NBR_Q), BF16)],
        compiler_params=_params("parallel"),
        name="nbr_proj",
    )(x, w, wvt)


def _nbr_kernel(var_ref, q_ref, *refs):
    del var_ref
    k_refs = refs[:NBR_KBLK]
    vt_refs = refs[NBR_KBLK:2 * NBR_KBLK]
    bias_ref, o_ref = refs[2 * NBR_KBLK], refs[2 * NBR_KBLK + 1]
    k = jnp.concatenate([r[0] for r in k_refs], axis=1)
    vt = jnp.concatenate([r[0, :, 0] for r in vt_refs], axis=2)
    s = jnp.einsum('hkd,hqd->hkq', k, q_ref[0], preferred_element_type=F32) + bias_ref[0]
    m = jnp.max(s, axis=1, keepdims=True)
    p = jnp.exp(s - m).astype(BF16)
    acc = jnp.einsum('hdk,hkq->hdq', vt, p, preferred_element_type=F32)
    o = acc[:, :C_HEAD_DIM] * (1.0 / acc[:, C_HEAD_DIM:C_HEAD_DIM + 1])
    o_ref[0] = o.reshape(C_HEADS * C_HEAD_DIM, NBR_Q).T.astype(o_ref.dtype)


def _nbr_windows(rows):
    nkr = 2 * NBR_KBLK
    sigs, var_of_u = [], []
    for u in range(rows // 2):
        ks = min(max(2 * u - C_WIN_H // 2, 0), rows - nkr)
        sig = []
        for a in range(2):
            r = 2 * u + a
            rs = min(max(r - C_WIN_H // 2, 0), rows - C_WIN_H)
            sig.append((ks - r, ks - rs))
        sig = tuple(sig)
        if sig not in sigs:
            sigs.append(sig)
        var_of_u.append(sigs.index(sig))
    row_idx = [[[min(max(i + dr + C_WIN_H - 1, 0), 2 * C_WIN_H - 2) for i in range(nkr)] for dr, _ in sig] for sig in sigs]
    inside = [[[0 <= i + ds < C_WIN_H for i in range(nkr)] for _, ds in sig] for sig in sigs]
    return var_of_u, row_idx, inside


def _nbr_bias(rpb, rows):
    kw, W = C_WIN_W, GRID_W
    var_of_u, row_idx, inside_row = _nbr_windows(rows)
    qc = jnp.arange(W)[:, None]
    kc = jnp.arange(W)[None, :]
    c_start = jnp.clip(qc - kw // 2, 0, W - kw)
    inside_col = (kc >= c_start) & (kc < c_start + kw)
    col_idx = jnp.clip(kc - qc + (kw - 1), 0, 2 * kw - 2)
    row_sel = jax.nn.one_hot(jnp.asarray(row_idx, jnp.int32), 2 * C_WIN_H - 1, dtype=F32)
    col_sel = jax.nn.one_hot(col_idx, 2 * kw - 1, dtype=F32)
    b = jnp.einsum('hrc,vair->hvaic', rpb, row_sel, precision=lax.Precision.HIGHEST)
    b = jnp.einsum('hvaic,qkc->hvaiqk', b, col_sel, precision=lax.Precision.HIGHEST)
    ok = jnp.asarray(inside_row)[None, :, :, :, None, None] & inside_col[None, None, None, None]
    b = jnp.where(ok, b, NEG).transpose(1, 0, 3, 5, 2, 4)
    nv = len(row_idx)
    return b.reshape(nv, C_HEADS, 2 * NBR_KBLK * W, NBR_Q).astype(F32), jnp.asarray(var_of_u, jnp.int32)


def _nbr_attention(q, k, vt, bias, var_of_u):
    B, H, S, d = q.shape
    nu = S // NBR_Q

    def first(u):
        return jnp.clip(u - C_WIN_H // 4, 0, nu - NBR_KBLK)

    k_specs = [pl.BlockSpec((1, H, NBR_Q, d), functools.partial(lambda b, u, var, i: (b, 0, first(u) + i, 0), i=i))
               for i in range(NBR_KBLK)]
    vt_specs = [pl.BlockSpec((1, H, 1, V_ROWS, NBR_Q), functools.partial(lambda b, u, var, i: (b, 0, first(u) + i, 0, 0), i=i))
                for i in range(NBR_KBLK)]
    grid_spec = pltpu.PrefetchScalarGridSpec(
        num_scalar_prefetch=1,
        grid=(B, nu),
        in_specs=[pl.BlockSpec((1, H, NBR_Q, d), lambda b, u, var: (b, 0, u, 0))] + k_specs + vt_specs
        + [pl.BlockSpec((1, H, NBR_KBLK * NBR_Q, NBR_Q), lambda b, u, var: (var[u], 0, 0, 0))],
        out_specs=pl.BlockSpec((1, NBR_Q, H * d), lambda b, u, var: (b, u, 0)),
    )
    return pl.pallas_call(
        _nbr_kernel,
        grid_spec=grid_spec,
        out_shape=jax.ShapeDtypeStruct((B, S, H * d), BF16),
        compiler_params=_params("parallel", "arbitrary"),
        name="nbr_attention",
    )(var_of_u, q, *([k] * NBR_KBLK), *([vt] * NBR_KBLK), bias)


def _out_ln_kernel(a_ref, w_ref, x_ref, g_ref, b_ref, o_ref):
    h = jnp.dot(a_ref[...], w_ref[...], preferred_element_type=F32)
    o_ref[...] = _layer_norm(DEEPNORM_ALPHA * x_ref[...] + h, g_ref[...], b_ref[...])


def _out_ln(a, w, x, g, b, bm):
    N = x.shape[0]
    row = lambda i: (i, 0)
    return pl.pallas_call(
        _out_ln_kernel,
        grid=(N // bm,),
        in_specs=[pl.BlockSpec((bm, a.shape[1]), row), _full(w.shape), pl.BlockSpec((bm, D_MODEL), row),
                  _full(g.shape), _full(b.shape)],
        out_specs=pl.BlockSpec((bm, D_MODEL), row),
        out_shape=jax.ShapeDtypeStruct((N, D_MODEL), F32),
        compiler_params=_params("parallel"),
        name="out_proj_ln",
    )(a, w, x, g, b)


def _swiglu_up_kernel(x_ref, wg_ref, wu_ref, h_ref):
    xb = x_ref[...].astype(BF16)
    g = jnp.dot(xb, wg_ref[...], preferred_element_type=F32)
    u = jnp.dot(xb, wu_ref[...], preferred_element_type=F32)
    h_ref[...] = (g * jax.nn.sigmoid(g) * u).astype(BF16)


def _swiglu_up(x, wg, wu, bm, bn):
    N = x.shape[0]
    F = wg.shape[1]
    return pl.pallas_call(
        _swiglu_up_kernel,
        grid=(F // bn, N // bm),
        in_specs=[pl.BlockSpec((bm, D_MODEL), lambda j, i: (i, 0)), pl.BlockSpec((D_MODEL, bn), lambda j, i: (0, j)),
                  pl.BlockSpec((D_MODEL, bn), lambda j, i: (0, j))],
        out_specs=pl.BlockSpec((bm, bn), lambda j, i: (i, j)),
        out_shape=jax.ShapeDtypeStruct((N, F), BF16),
        compiler_params=_params("parallel", "parallel"),
        name="swiglu_up",
    )(x, wg, wu)


def _ple(x, p_ref, wpg_ref, wpi_ref):
    gate = jax.nn.sigmoid(jnp.dot(x.astype(BF16), wpg_ref[...], preferred_element_type=F32))
    return gate * jnp.dot(p_ref[0].astype(BF16), wpi_ref[...], preferred_element_type=F32)


def _ffn_down_ln_kernel(h_ref, wd_ref, x_ref, p_ref, wpg_ref, wpi_ref, g_ref, b_ref, o_ref):
    x = x_ref[...]
    f = jnp.dot(h_ref[...], wd_ref[...], preferred_element_type=F32)
    o_ref[...] = _layer_norm(DEEPNORM_ALPHA * x + f + _ple(x, p_ref, wpg_ref, wpi_ref), g_ref[...], b_ref[...])


def _ffn_down_ln(h, wd, x, p, layer, wpg, wpi, g, b, bm):
    N = x.shape[0]
    row = lambda i: (i, 0)
    return pl.pallas_call(
        _ffn_down_ln_kernel,
        grid=(N // bm,),
        in_specs=[pl.BlockSpec((bm, h.shape[1]), row), _full(wd.shape), pl.BlockSpec((bm, D_MODEL), row),
                  pl.BlockSpec((1, bm, PLE_DIM), lambda i: (layer, i, 0)), _full(wpg.shape), _full(wpi.shape),
                  _full(g.shape), _full(b.shape)],
        out_specs=pl.BlockSpec((bm, D_MODEL), row),
        out_shape=jax.ShapeDtypeStruct((N, D_MODEL), F32),
        compiler_params=_params("parallel"),
        name="ffn_down_ple_ln",
    )(h, wd, x, p, wpg, wpi, g, b)


def _router_kernel(x_ref, w_ref, idx_ref, gate_ref, xb_ref):
    x = x_ref[...]
    xb_ref[...] = x.astype(BF16)
    logits = jnp.dot(x, w_ref[...], preferred_element_type=F32, precision=lax.Precision.HIGHEST)
    lane = lax.broadcasted_iota(jnp.int32, logits.shape, 1).astype(F32)
    logits = jnp.where(lane < N_EXPERTS, logits, NEG)
    m1 = jnp.max(logits, axis=-1, keepdims=True)
    i1 = jnp.min(jnp.where(logits == m1, lane, 128.0), axis=-1, keepdims=True)
    rest = jnp.where(lane == i1, NEG, logits)
    m2 = jnp.max(rest, axis=-1, keepdims=True)
    i2 = jnp.min(jnp.where(rest == m2, lane, 128.0), axis=-1, keepdims=True)
    e = jnp.exp(m2 - m1)
    g1 = 1.0 / (1.0 + e)
    idx_ref[...] = jnp.where(lane == 0.0, i1, jnp.where(lane == 1.0, i2, 0.0)).astype(jnp.int32)
    gate_ref[...] = jnp.where(lane == 0.0, g1, jnp.where(lane == 1.0, e * g1, 0.0))


def _router(x, w, bm):
    N = x.shape[0]
    row = lambda i: (i, 0)
    return pl.pallas_call(
        _router_kernel,
        grid=(N // bm,),
        in_specs=[pl.BlockSpec((bm, D_MODEL), row), _full(w.shape)],
        out_specs=[pl.BlockSpec((bm, 128), row)] * 2 + [pl.BlockSpec((bm, D_MODEL), row)],
        out_shape=[jax.ShapeDtypeStruct((N, 128), jnp.int32), jax.ShapeDtypeStruct((N, 128), F32),
                   jax.ShapeDtypeStruct((N, D_MODEL), BF16)],
        compiler_params=_params("parallel"),
        name="moe_router",
    )(x, w)


def _moe_ffn_kernel(te_ref, nt_ref, xs_ref, wg_ref, wu_ref, wd_ref, o_ref, acc_ref):
    t = pl.program_id(0)
    j = pl.program_id(1)

    @pl.when(t < nt_ref[0])
    def _():
        xs = xs_ref[...]
        g = jnp.dot(xs, wg_ref[0, 0], preferred_element_type=F32)
        u = jnp.dot(xs, wu_ref[0, 0], preferred_element_type=F32)
        h = (g * jax.nn.sigmoid(g) * u).astype(BF16)
        y = jnp.dot(h, wd_ref[0, 0], preferred_element_type=F32)

        @pl.when(j == 0)
        def _():
            acc_ref[...] = y

        @pl.when(j == pl.num_programs(1) - 1)
        def _():
            o_ref[...] = (acc_ref[...] + y).astype(o_ref.dtype)

    @pl.when(t >= nt_ref[0])
    def _():
        o_ref[...] = jnp.zeros_like(o_ref)


def _moe_ffn(tile_expert, n_tiles, xs, wg, wu, wd, layer, tm, halves=2):
    P = xs.shape[0]
    fh = FF_EXPERT // halves
    grid_spec = pltpu.PrefetchScalarGridSpec(
        num_scalar_prefetch=2,
        grid=(P // tm, halves),
        in_specs=[pl.BlockSpec((tm, D_MODEL), lambda t, j, te, nt: (t, 0)),
                  pl.BlockSpec((1, 1, D_MODEL, fh), lambda t, j, te, nt: (layer, te[t], 0, j)),
                  pl.BlockSpec((1, 1, D_MODEL, fh), lambda t, j, te, nt: (layer, te[t], 0, j)),
                  pl.BlockSpec((1, 1, fh, D_MODEL), lambda t, j, te, nt: (layer, te[t], j, 0))],
        out_specs=pl.BlockSpec((tm, D_MODEL), lambda t, j, te, nt: (t, 0)),
        scratch_shapes=[pltpu.VMEM((tm, D_MODEL), F32)],
    )
    return pl.pallas_call(
        _moe_ffn_kernel,
        grid_spec=grid_spec,
        out_shape=jax.ShapeDtypeStruct((P, D_MODEL), BF16),
        compiler_params=_params("arbitrary", "arbitrary"),
        name="moe_expert_ffn",
    )(tile_expert, n_tiles, xs, wg, wu, wd)


def _moe_combine_ln_kernel(ya_ref, yb_ref, gate_ref, x_ref, p_ref, wpg_ref, wpi_ref, g_ref, b_ref, o_ref):
    x = x_ref[...]
    gate = gate_ref[...]
    f = gate[:, 0:1] * ya_ref[...] + gate[:, 1:2] * yb_ref[...]
    o_ref[...] = _layer_norm(DEEPNORM_ALPHA * x + f + _ple(x, p_ref, wpg_ref, wpi_ref), g_ref[...], b_ref[...])


def _moe_combine_ln(ya, yb, gates, x, p, layer, wpg, wpi, g, b, bm):
    N = x.shape[0]
    row = lambda i: (i, 0)
    rows = pl.BlockSpec((bm, D_MODEL), row)
    return pl.pallas_call(
        _moe_combine_ln_kernel,
        grid=(N // bm,),
        in_specs=[rows, rows, pl.BlockSpec((bm, 128), row), rows, pl.BlockSpec((1, bm, PLE_DIM), lambda i: (layer, i, 0)),
                  _full(wpg.shape), _full(wpi.shape), _full(g.shape), _full(b.shape)],
        out_specs=rows,
        out_shape=jax.ShapeDtypeStruct((N, D_MODEL), F32),
        compiler_params=_params("parallel"),
        name="moe_combine_ple_ln",
    )(ya, yb, gates, x, p, wpg, wpi, g, b)


def _route(idx, tm):
    N = idx.shape[0]
    e_flat = idx.reshape(-1)
    onehot = (e_flat[:, None] == jnp.arange(N_EXPERTS, dtype=jnp.int32)[None, :]).astype(jnp.int32)
    csum = jnp.cumsum(onehot, axis=0)
    rank = jnp.sum(csum * onehot, axis=1) - 1
    counts = csum[-1]
    padded = ((counts + tm - 1) // tm) * tm
    ends = jnp.cumsum(padded)
    slot = (ends - padded)[e_flat] + rank
    P = TOP_K * N + N_EXPERTS * tm
    row_token = jnp.zeros((P,), jnp.int32).at[slot].set(
        jnp.arange(TOP_K * N, dtype=jnp.int32) // TOP_K, unique_indices=True, mode="promise_in_bounds")
    tile_start = jnp.arange(P // tm, dtype=jnp.int32) * tm
    tile_expert = jnp.minimum(jnp.sum((tile_start[:, None] >= ends[None, :]).astype(jnp.int32), axis=1), N_EXPERTS - 1)
    n_tiles = (ends[-1] // tm).astype(jnp.int32).reshape(1)
    return slot.reshape(N, TOP_K), row_token, tile_expert.astype(jnp.int32), n_tiles


def _swap_halves(w, group):
    shp = w.shape
    w = w.reshape(shp[:-1] + (shp[-1] // group, 2, group // 2))
    return jnp.flip(w, axis=-2).reshape(shp)


def _rope_tables(pos, dim):
    inv = ROPE_THETA ** (-jnp.arange(0, dim, 2, dtype=F32) / dim)
    ang = pos.astype(F32)[:, None] * inv[None, :]
    c, s = jnp.cos(ang), jnp.sin(ang)
    return jnp.concatenate([c, c], -1), jnp.concatenate([-s, s], -1)


def _mla_prep(w_dq, w_dkv, w_uq, w_ukv, S):
    wd = jnp.zeros((D_MODEL, 896), F32)
    wd = wd.at[:, :A_Q_LORA].set(w_dq).at[:, A_Q_LORA:640].set(w_dkv[:, :A_KV_LORA])
    w_r = w_dkv[:, A_KV_LORA:]
    wd = wd.at[:, 640 + A_NOPE:640 + A_NOPE + A_ROPE].set(w_r)
    wd = wd.at[:, 768 + A_NOPE:768 + A_NOPE + A_ROPE].set(_swap_halves(w_r, A_ROPE))
    uq = w_uq.reshape(A_Q_LORA, A_HEADS, A_NOPE + A_ROPE)
    w1 = jnp.pad(uq, ((0, 0), (0, 0), (0, A_DK - A_NOPE - A_ROPE))).reshape(A_Q_LORA, A_HEADS * A_DK)
    w2 = jnp.pad(_swap_halves(uq[..., A_NOPE:], A_ROPE), ((0, 0), (0, 0), (A_NOPE, A_DK - A_NOPE - A_ROPE)))
    w2 = w2.reshape(A_Q_LORA, A_HEADS * A_DK)
    ukv = w_ukv.reshape(A_KV_LORA, A_HEADS, A_NOPE + A_V)
    wk = jnp.pad(ukv[..., :A_NOPE], ((0, 0), (0, 0), (0, A_DK - A_NOPE))).reshape(A_KV_LORA, A_HEADS * A_DK)
    wvt = ukv[..., A_NOPE:].reshape(A_KV_LORA, A_HEADS * A_V).T
    c, s = _rope_tables(jnp.arange(S), A_ROPE)
    pad = lambda t, fill: jnp.concatenate([jnp.full((S, A_NOPE), fill, F32), t, jnp.zeros((S, A_DK - A_NOPE - A_ROPE), F32)], -1)
    scale = (A_NOPE + A_ROPE) ** -0.5 * LOG2E
    tabs = dict(k_cos=pad(c, 0.0), k_sin=pad(s, 0.0), q_cos=pad(c, 1.0) * scale, q_sin=pad(s, 0.0) * scale)
    return wd.astype(BF16), w1.astype(BF16), w2.astype(BF16), wk.astype(BF16), wvt.astype(BF16), tabs


def _gqa_prep(w_qkv, g_q, g_k, S):
    nq = B_Q_HEADS * B_HEAD_DIM
    nk = B_KV_HEADS * B_HEAD_DIM
    half = B_HEAD_DIM // 2
    wq, wk, wv = w_qkv[:, :nq], w_qkv[:, nq:nq + nk], w_qkv[:, nq + nk:]
    wkk = jnp.concatenate([wk, _swap_halves(wk, half)], axis=1)
    t = jnp.arange(S)
    cr, sr = _rope_tables(t // GRID_W, half)
    cc, sc = _rope_tables(t % GRID_W, half)
    cos_t = jnp.tile(jnp.concatenate([cr, cc], -1), (1, 2))
    sin_t = jnp.tile(jnp.concatenate([sr, sc], -1), (1, 2))
    lane = jnp.arange(nq) // B_HEAD_DIM
    ones_bd = (lane[:, None] == lane[None, :]).astype(BF16)
    gq = jnp.tile(g_q, B_Q_HEADS)[None, :]
    gk = jnp.tile(g_k, B_KV_HEADS)[None, :]
    return (wq.astype(BF16), _swap_halves(wq, half).astype(BF16), wkk.astype(BF16), wv.T.astype(BF16), ones_bd,
            gq, _swap_halves(gq, half), gk, _swap_halves(gk, half), cos_t, sin_t)


def _mixer(i, x, B, S, w, bm):
    kind, j = i % N_MIXERS, i // N_MIXERS
    if kind == 0:
        wd, w1, w2, wk, wvt, tabs = _mla_prep(w['a_w_dq'][j], w['a_w_dkv'][j], w['a_w_uq'][j], w['a_w_ukv'][j], S)
        cq, ckv, kr = _mla_down(x, wd, w['a_g_q'][j][None, :], w['a_g_kv'][j][None, :], tabs['k_cos'], tabs['k_sin'], S, bm)
        q = _mla_q_up(cq, w1, w2, tabs['q_cos'], tabs['q_sin'], B, S, bm)
        k, vt = _mla_kv_up(ckv, kr, wk, wvt, B, S, min(KV_CHUNK, S))
        a = _flash(q, k, vt, hp=2, G=1, tq=min(512, S), nbuf=4)
        w_o = w['a_w_o'][j]
    elif kind == 1:
        prep = _gqa_prep(w['b_w_qkv'][j], w['b_g_q'][j], w['b_g_k'][j], S)
        q, k, vt = _gqa_proj(x, *prep, B, S, bm)
        a = _flash(q, k, vt, hp=1, G=B_Q_HEADS // B_KV_HEADS, tq=min(256, S), nbuf=2)
        w_o = w['b_w_o'][j]
    else:
        n = C_HEADS * C_HEAD_DIM
        w_qkv = w['c_w_qkv'][j]
        q, k, vt = _nbr_proj(x, w_qkv[:, :2 * n].astype(BF16), w_qkv[:, 2 * n:].T.astype(BF16), B, S, bm)
        a = _nbr_attention(q, k, vt, *_nbr_bias(w['c_rpb'][j], S // GRID_W))
        w_o = w['c_w_o'][j]
    return a.reshape(B * S, D_MODEL), w_o.astype(BF16)


def _trunk(groups, w):
    shapes = [x.shape[:2] for x, _ in groups]
    xs = [x.reshape(B * S, D_MODEL) for (x, _), (B, S) in zip(groups, shapes)]
    ps = [p.reshape(p.shape[0], B * S, PLE_DIM) for (_, p), (B, S) in zip(groups, shapes)]
    bms = [min(ROW_BLOCK, S) for _, S in shapes]
    for i in range(DEPTH):
        for g, (B, S) in enumerate(shapes):
            a, w_o = _mixer(i, xs[g], B, S, w, bms[g])
            xs[g] = _out_ln(a, w_o, xs[g], w['ln1_g'][i][None, :], w['ln1_b'][i][None, :], bms[g])
        f_i = i // 2
        wpg, wpi = w['ple_w_gate'][i].astype(BF16), w['ple_w_in'][i].astype(BF16)
        g2, b2 = w['ln2_g'][i][None, :], w['ln2_b'][i][None, :]
        if i % 2 == 0:
            wg, wu, wd = (w[k][f_i].astype(BF16) for k in ('f_w_gate', 'f_w_up', 'f_w_down'))
            for g in range(len(xs)):
                h = _swiglu_up(xs[g], wg, wu, bms[g], FF_DENSE // 2)
                xs[g] = _ffn_down_ln(h, wd, xs[g], ps[g], i, wpg, wpi, g2, b2, bms[g])
        else:
            w_r = jnp.pad(w['m_w_router'][f_i], ((0, 0), (0, 128 - N_EXPERTS)))
            routed = [_router(x, w_r, bm) for x, bm in zip(xs, bms)]
            idx = jnp.concatenate([r[0][:, :TOP_K] for r in routed], axis=0)
            xb = jnp.concatenate([r[2] for r in routed], axis=0)
            slot, row_token, tile_expert, n_tiles = _route(idx, MOE_TILE)
            rows = xb.at[row_token].get(mode="promise_in_bounds")
            ys = _moe_ffn(tile_expert, n_tiles, rows, w['m_w_gate'].astype(BF16), w['m_w_up'].astype(BF16),
                          w['m_w_down'].astype(BF16), f_i, MOE_TILE)
            off = 0
            for g in range(len(xs)):
                n = xs[g].shape[0]
                ya = ys.at[slot[off:off + n, 0]].get(mode="promise_in_bounds")
                yb = ys.at[slot[off:off + n, 1]].get(mode="promise_in_bounds")
                xs[g] = _moe_combine_ln(ya, yb, routed[g][1], xs[g], ps[g], i, wpg, wpi, g2, b2, bms[g])
                off += n
    return tuple(x.reshape(B, S, D_MODEL) for x, (B, S) in zip(xs, shapes))


def kernel(x_prompt, x_sample, p_prompt, p_sample, a_w_dq, a_g_q, a_w_uq, a_w_dkv, a_g_kv, a_w_ukv, a_w_o, b_w_qkv, b_g_q, b_g_k, b_w_o, c_w_qkv, c_rpb, c_w_o, ln1_g, ln1_b, ln2_g, ln2_b, f_w_gate, f_w_up, f_w_down, m_w_router, m_w_gate, m_w_up, m_w_down, ple_w_gate, ple_w_in):
    w = dict(a_w_dq=a_w_dq, a_g_q=a_g_q, a_w_uq=a_w_uq, a_w_dkv=a_w_dkv, a_g_kv=a_g_kv, a_w_ukv=a_w_ukv, a_w_o=a_w_o,
             b_w_qkv=b_w_qkv, b_g_q=b_g_q, b_g_k=b_g_k, b_w_o=b_w_o, c_w_qkv=c_w_qkv, c_rpb=c_rpb, c_w_o=c_w_o,
             ln1_g=ln1_g, ln1_b=ln1_b, ln2_g=ln2_g, ln2_b=ln2_b, f_w_gate=f_w_gate, f_w_up=f_w_up, f_w_down=f_w_down,
             m_w_router=m_w_router, m_w_gate=m_w_gate, m_w_up=m_w_up, m_w_down=m_w_down,
             ple_w_gate=ple_w_gate, ple_w_in=ple_w_in)
    return _trunk([(x_prompt, p_prompt), (x_sample, p_sample)], w)
```

```python
import functools

import jax
import jax.numpy as jnp
from jax import lax
from jax.experimental import pallas as pl
from jax.experimental.pallas import tpu as pltpu

F32 = jnp.float32
BF16 = jnp.bfloat16

D_MODEL = 1024
DEPTH = 4
GRID_W = 64
PLE_DIM = 256
N_MIXERS = 3
ROPE_THETA = 10000.0
RMS_EPS = 1e-6
LN_EPS = 1e-5
DEEPNORM_ALPHA = (2 * DEPTH) ** 0.25

A_HEADS = 16
A_Q_LORA = 384
A_KV_LORA = 256
A_NOPE = 64
A_ROPE = 32
A_V = 64
A_DK = 128
B_Q_HEADS = 16
B_KV_HEADS = 4
B_HEAD_DIM = 64
C_HEADS = 16
C_HEAD_DIM = 64
C_WIN_H = 8
C_WIN_W = 16
FF_DENSE = 2816
N_EXPERTS = 8
TOP_K = 2
FF_EXPERT = 3584

V_ROWS = 80
NEG = -1e30
LOG2E = 1.4426950408889634
VMEM_LIMIT = 56 * 1024 * 1024
ROW_BLOCK = 512
KV_CHUNK = 512
MOE_TILE = 512
CAST_BLOCK_ELEMS = 7 * 2 ** 18
NBR_Q = 2 * GRID_W
NBR_KBLK = (C_WIN_H + 2) // 2
NT = (((1,), (1,)), ((), ()))


def _params(*sem):
    return pltpu.CompilerParams(dimension_semantics=sem, vmem_limit_bytes=VMEM_LIMIT)


def _full(shape):
    return pl.BlockSpec(shape, lambda *_: (0,) * len(shape))


def _rms(x, g):
    return x * lax.rsqrt(jnp.mean(x * x, axis=-1, keepdims=True) + RMS_EPS) * g


def _layer_norm(x, g, b):
    mu = jnp.mean(x, axis=-1, keepdims=True)
    xc = x - mu
    var = jnp.mean(xc * xc, axis=-1, keepdims=True)
    return xc * lax.rsqrt(var + LN_EPS) * g + b


def _tile_lanes(t, n):
    return jnp.concatenate([t] * n, axis=-1)


def _mla_down_kernel(x_ref, w_ref, gq_ref, gkv_ref, cos_ref, sin_ref, cq_ref, ckv_ref, kr_ref):
    acc = jnp.dot(x_ref[...].astype(BF16), w_ref[...], preferred_element_type=F32)
    cq_ref[...] = _rms(acc[:, :A_Q_LORA], gq_ref[...]).astype(BF16)
    ckv_ref[...] = _rms(acc[:, A_Q_LORA:640], gkv_ref[...]).astype(BF16)
    kr_ref[...] = (acc[:, 640:768] * cos_ref[...] + acc[:, 768:896] * sin_ref[...]).astype(BF16)


def _mla_down(x, w, gq, gkv, cos_t, sin_t, S, bm):
    N = x.shape[0]
    nsb = S // bm
    row = lambda i: (i, 0)
    tab = lambda i: (i % nsb, 0)
    return pl.pallas_call(
        _mla_down_kernel,
        grid=(N // bm,),
        in_specs=[pl.BlockSpec((bm, D_MODEL), row), _full(w.shape), _full(gq.shape), _full(gkv.shape),
                  pl.BlockSpec((bm, 128), tab), pl.BlockSpec((bm, 128), tab)],
        out_specs=[pl.BlockSpec((bm, A_Q_LORA), row), pl.BlockSpec((bm, A_KV_LORA), row),
                   pl.BlockSpec((bm, 128), row)],
        out_shape=[jax.ShapeDtypeStruct((N, A_Q_LORA), BF16), jax.ShapeDtypeStruct((N, A_KV_LORA), BF16),
                   jax.ShapeDtypeStruct((N, 128), BF16)],
        compiler_params=_params("parallel"),
        name="mla_down",
    )(x, w, gq, gkv, cos_t, sin_t)


def _mla_q_up_kernel(cq_ref, w1_ref, w2_ref, cos_ref, sin_ref, q_ref):
    cq = cq_ref[...]
    a1 = jnp.dot(cq, w1_ref[...], preferred_element_type=F32)
    a2 = jnp.dot(cq, w2_ref[...], preferred_element_type=F32)
    q = a1 * _tile_lanes(cos_ref[...], A_HEADS) + a2 * _tile_lanes(sin_ref[...], A_HEADS)
    for h in range(A_HEADS):
        q_ref[0, h] = q[:, h * A_DK:(h + 1) * A_DK].astype(BF16)


def _mla_q_up(cq, w1, w2, cos_t, sin_t, B, S, bm):
    nsb = S // bm
    return pl.pallas_call(
        _mla_q_up_kernel,
        grid=(B * nsb,),
        in_specs=[pl.BlockSpec((bm, A_Q_LORA), lambda i: (i, 0)), _full(w1.shape), _full(w2.shape),
                  pl.BlockSpec((bm, 128), lambda i: (i % nsb, 0)), pl.BlockSpec((bm, 128), lambda i: (i % nsb, 0))],
        out_specs=pl.BlockSpec((1, A_HEADS, bm, A_DK), lambda i: (i // nsb, 0, i % nsb, 0)),
        out_shape=jax.ShapeDtypeStruct((B, A_HEADS, S, A_DK), BF16),
        compiler_params=_params("parallel"),
        name="mla_q_up",
    )(cq, w1, w2, cos_t, sin_t)


def _ones_rows(n):
    r = lax.broadcasted_iota(jnp.int32, (V_ROWS - 64, n), 0)
    return jnp.where(r == 0, 1.0, 0.0).astype(BF16)


def _mla_kv_up_kernel(ckv_ref, kr_ref, wk_ref, wvt_ref, k_ref, vt_ref):
    ckv = ckv_ref[...]
    ak = jnp.dot(ckv, wk_ref[...], preferred_element_type=F32)
    kr = kr_ref[...].astype(F32)
    avt = lax.dot_general(wvt_ref[...], ckv, NT, preferred_element_type=F32)
    ones = _ones_rows(ckv.shape[0])
    for h in range(A_HEADS):
        k_ref[0, h] = (ak[:, h * A_DK:(h + 1) * A_DK] + kr).astype(BF16)
        vt_ref[0, h, 0, 0:A_V, :] = avt[h * A_V:(h + 1) * A_V, :].astype(BF16)
        vt_ref[0, h, 0, A_V:V_ROWS, :] = ones


def _mla_kv_up(ckv, kr, wk, wvt, B, S, bm):
    nsb = S // bm
    return pl.pallas_call(
        _mla_kv_up_kernel,
        grid=(B * nsb,),
        in_specs=[pl.BlockSpec((bm, A_KV_LORA), lambda i: (i, 0)), pl.BlockSpec((bm, 128), lambda i: (i, 0)),
                  _full(wk.shape), _full(wvt.shape)],
        out_specs=[pl.BlockSpec((1, A_HEADS, bm, A_DK), lambda i: (i // nsb, 0, i % nsb, 0)),
                   pl.BlockSpec((1, A_HEADS, 1, V_ROWS, bm), lambda i: (i // nsb, 0, i % nsb, 0, 0))],
        out_shape=[jax.ShapeDtypeStruct((B, A_HEADS, S, A_DK), BF16),
                   jax.ShapeDtypeStruct((B, A_HEADS, nsb, V_ROWS, bm), BF16)],
        compiler_params=_params("parallel"),
        name="mla_kv_up",
    )(ckv, kr, wk, wvt)


def _gqa_proj_kernel(x_ref, wq_ref, wqp_ref, wk_ref, wvt_ref, ones_ref, gq_ref, gqp_ref, gk_ref, gkp_ref,
                     cos_ref, sin_ref, q_ref, k_ref, vt_ref):
    xb = x_ref[...].astype(BF16)
    nq = B_Q_HEADS * B_HEAD_DIM
    nk = B_KV_HEADS * B_HEAD_DIM
    cos = cos_ref[...]
    sin = sin_ref[...]
    ones_bd = ones_ref[...]

    def norm_rope(a, ap, g, gp, n):
        ss = jnp.dot((a * a).astype(BF16), ones_bd[:n, :n], preferred_element_type=F32) * (1.0 / B_HEAD_DIM)
        r = lax.rsqrt(ss + RMS_EPS)
        reps = n // 128
        return r * (a * g * _tile_lanes(cos, reps) + ap * gp * _tile_lanes(sin, reps))

    aq = jnp.dot(xb, wq_ref[...], preferred_element_type=F32)
    aqp = jnp.dot(xb, wqp_ref[...], preferred_element_type=F32)
    q = norm_rope(aq, aqp, gq_ref[...], gqp_ref[...], nq) * (B_HEAD_DIM ** -0.5 * LOG2E)
    akk = jnp.dot(xb, wk_ref[...], preferred_element_type=F32)
    k = norm_rope(akk[:, :nk], akk[:, nk:], gk_ref[...], gkp_ref[...], nk)
    avt = lax.dot_general(wvt_ref[...], xb, NT, preferred_element_type=F32)
    ones = _ones_rows(xb.shape[0])
    for h in range(B_Q_HEADS):
        q_ref[0, h] = q[:, h * B_HEAD_DIM:(h + 1) * B_HEAD_DIM].astype(BF16)
    for h in range(B_KV_HEADS):
        k_ref[0, h] = k[:, h * B_HEAD_DIM:(h + 1) * B_HEAD_DIM].astype(BF16)
        vt_ref[0, h, 0, 0:B_HEAD_DIM, :] = avt[h * B_HEAD_DIM:(h + 1) * B_HEAD_DIM, :].astype(BF16)
        vt_ref[0, h, 0, B_HEAD_DIM:V_ROWS, :] = ones


def _gqa_proj(x, wq, wqp, wk, wvt, ones_bd, gq, gqp, gk, gkp, cos_t, sin_t, B, S, bm):
    nsb = S // bm
    hm = lambda i: (i // nsb, 0, i % nsb, 0)
    tab = lambda i: (i % nsb, 0)
    ins = [x, wq, wqp, wk, wvt, ones_bd, gq, gqp, gk, gkp, cos_t, sin_t]
    specs = [pl.BlockSpec((bm, D_MODEL), lambda i: (i, 0))] + [_full(a.shape) for a in ins[1:10]]
    specs += [pl.BlockSpec((bm, 128), tab), pl.BlockSpec((bm, 128), tab)]
    return pl.pallas_call(
        _gqa_proj_kernel,
        grid=(B * nsb,),
        in_specs=specs,
        out_specs=[pl.BlockSpec((1, B_Q_HEADS, bm, B_HEAD_DIM), hm), pl.BlockSpec((1, B_KV_HEADS, bm, B_HEAD_DIM), hm),
                   pl.BlockSpec((1, B_KV_HEADS, 1, V_ROWS, bm), lambda i: (i // nsb, 0, i % nsb, 0, 0))],
        out_shape=[jax.ShapeDtypeStruct((B, B_Q_HEADS, S, B_HEAD_DIM), BF16),
                   jax.ShapeDtypeStruct((B, B_KV_HEADS, S, B_HEAD_DIM), BF16),
                   jax.ShapeDtypeStruct((B, B_KV_HEADS, nsb, V_ROWS, bm), BF16)],
        compiler_params=_params("parallel"),
        name="gqa_proj",
    )(*ins)


def _flash_kernel(q_ref, k_ref, vt_ref, o_ref, *scratch, hp, G, tq, tk, nk, dv, unroll, nbuf):
    s_bufs, acc_ref = scratch[:nbuf], scratch[nbuf]
    TQ = G * tq
    dk = q_ref.shape[-1]
    outs = []
    for h in range(hp):
        q = q_ref[0, h * G:(h + 1) * G].reshape(TQ, dk)

        def scores(c, s_ref, h=h, q=q):
            off = c * tk if isinstance(c, int) else pl.multiple_of(c * tk, tk)
            s = lax.dot_general(k_ref[0, h, pl.ds(off, tk), :], q, NT, preferred_element_type=F32)
            s_ref[...] = s
            return jnp.max(s, axis=0, keepdims=True)

        def step(c, i, m, a_cur, last, h=h):
            if not last:
                cm = scores(c + 1, s_bufs[(i + 1) % nbuf])
            p = jnp.exp2((s_bufs[i % nbuf][...] - m).astype(BF16))
            acc_ref[...] = a_cur * acc_ref[...] + jnp.dot(vt_ref[0, h, c], p, preferred_element_type=F32)
            if last:
                return m, a_cur
            m_new = jnp.maximum(m, cm)
            return m_new, jnp.exp2(m - m_new)

        def group(jj, carry, last=False):
            m, a_cur = carry
            for i in range(unroll):
                m, a_cur = step(unroll * jj + i, i, m, a_cur, last and i == unroll - 1)
            return m, a_cur

        acc_ref[...] = jnp.zeros_like(acc_ref)
        m0 = jnp.maximum(scores(0, s_bufs[0]), NEG)
        ng = nk // unroll
        carry = (m0, jnp.ones((1, TQ), F32))
        if ng > 1:
            carry = lax.fori_loop(0, ng - 1, group, carry)
        group(ng - 1, carry, last=True)
        acc = acc_ref[...]
        o = acc[:dv] * (1.0 / acc[dv:dv + 1])
        for g in range(G):
            outs.append(o[:, g * tq:(g + 1) * tq])
    o_ref[0] = jnp.concatenate(outs, axis=0).T.astype(o_ref.dtype)


def _flash(q, k, vt, *, hp, G, tq, nbuf, dv=64):
    B, H, S, dk = q.shape
    Hk = k.shape[1]
    nk, tk = vt.shape[2], vt.shape[4]
    unroll = 8 if nk % 8 == 0 and nk > 8 else 4 if nk % 4 == 0 else 2
    nbuf = min(nbuf, unroll)
    assert nk % unroll == 0 and unroll % nbuf == 0
    TQ = G * tq
    scratch = [pltpu.VMEM((tk, TQ), F32)] * nbuf + [pltpu.VMEM((V_ROWS, TQ), F32)]
    kern = functools.partial(_flash_kernel, hp=hp, G=G, tq=tq, tk=tk, nk=nk, dv=dv, unroll=unroll, nbuf=nbuf)
    return pl.pallas_call(
        kern,
        grid=(B, Hk // hp, S // tq),
        in_specs=[pl.BlockSpec((1, hp * G, tq, dk), lambda b, h, i: (b, h, i, 0)),
                  pl.BlockSpec((1, hp, S, dk), lambda b, h, i: (b, h, 0, 0)),
                  pl.BlockSpec((1, hp, nk, V_ROWS, tk), lambda b, h, i: (b, h, 0, 0, 0))],
        out_specs=pl.BlockSpec((1, tq, hp * G * dv), lambda b, h, i: (b, i, h)),
        out_shape=jax.ShapeDtypeStruct((B, S, H * dv), BF16),
        scratch_shapes=scratch,
        compiler_params=_params("parallel", "parallel", "arbitrary"),
        name="flash_attention",
    )(q, k, vt)


def _nbr_proj_kernel(x_ref, w_ref, wvt_ref, q_ref, k_ref, vt_ref):
    xb = x_ref[...].astype(BF16)
    n = C_HEADS * C_HEAD_DIM
    acc = jnp.dot(xb, w_ref[...], preferred_element_type=F32)
    avt = lax.dot_general(wvt_ref[...], xb, NT, preferred_element_type=F32)
    ones = _ones_rows(NBR_Q)
    for h in range(C_HEADS):
        lo = h * C_HEAD_DIM
        q_ref[0, h] = (acc[:, lo:lo + C_HEAD_DIM] * (C_HEAD_DIM ** -0.5)).astype(BF16)
        k_ref[0, h] = acc[:, n + lo:n + lo + C_HEAD_DIM].astype(BF16)
        for c in range(xb.shape[0] // NBR_Q):
            vt_ref[0, h, c, 0:C_HEAD_DIM, :] = avt[lo:lo + C_HEAD_DIM, c * NBR_Q:(c + 1) * NBR_Q].astype(BF16)
            vt_ref[0, h, c, C_HEAD_DIM:V_ROWS, :] = ones


def _nbr_proj(x, w, wvt, B, S, bm):
    nsb = S // bm
    hm = lambda i: (i // nsb, 0, i % nsb, 0)
    shp = jax.ShapeDtypeStruct((B, C_HEADS, S, C_HEAD_DIM), BF16)
    return pl.pallas_call(
        _nbr_proj_kernel,
        grid=(B * nsb,),
        in_specs=[pl.BlockSpec((bm, D_MODEL), lambda i: (i, 0)), _full(w.shape), _full(wvt.shape)],
        out_specs=[pl.BlockSpec((1, C_HEADS, bm, C_HEAD_DIM), hm)] * 2
        + [pl.BlockSpec((1, C_HEADS, bm // NBR_Q, V_ROWS, NBR_Q), lambda i: (i // nsb, 0, i % nsb, 0, 0))],
        out_shape=[shp, shp, jax.ShapeDtypeStruct((B, C_HEADS, S // NBR_Q, V_ROWS, NBR_Q), BF16)],
        compiler_params=_params("parallel"),
        name="nbr_proj",
    )(x, w, wvt)


def _nbr_kernel(var_ref, q_ref, *refs):
    del var_ref
    k_refs = refs[:NBR_KBLK]
    vt_refs = refs[NBR_KBLK:2 * NBR_KBLK]
    bias_ref, o_ref = refs[2 * NBR_KBLK], refs[2 * NBR_KBLK + 1]
    k = jnp.concatenate([r[0] for r in k_refs], axis=1)
    vt = jnp.concatenate([r[0, :, 0] for r in vt_refs], axis=2)
    s = jnp.einsum('hkd,hqd->hkq', k, q_ref[0], preferred_element_type=F32) + bias_ref[0]
    m = jnp.max(s, axis=1, keepdims=True)
    p = jnp.exp(s - m).astype(BF16)
    acc = jnp.einsum('hdk,hkq->hdq', vt, p, preferred_element_type=F32)
    o = acc[:, :C_HEAD_DIM] * (1.0 / acc[:, C_HEAD_DIM:C_HEAD_DIM + 1])
    o_ref[0] = o.reshape(C_HEADS * C_HEAD_DIM, NBR_Q).T.astype(o_ref.dtype)


def _nbr_windows(rows):
    nkr = 2 * NBR_KBLK
    sigs, var_of_u = [], []
    for u in range(rows // 2):
        ks = min(max(2 * u - C_WIN_H // 2, 0), rows - nkr)
        sig = []
        for a in range(2):
            r = 2 * u + a
            rs = min(max(r - C_WIN_H // 2, 0), rows - C_WIN_H)
            sig.append((ks - r, ks - rs))
        sig = tuple(sig)
        if sig not in sigs:
            sigs.append(sig)
        var_of_u.append(sigs.index(sig))
    row_idx = [[[min(max(i + dr + C_WIN_H - 1, 0), 2 * C_WIN_H - 2) for i in range(nkr)] for dr, _ in sig] for sig in sigs]
    inside = [[[0 <= i + ds < C_WIN_H for i in range(nkr)] for _, ds in sig] for sig in sigs]
    return var_of_u, row_idx, inside


def _nbr_bias(rpb, rows):
    kw, W = C_WIN_W, GRID_W
    var_of_u, row_idx, inside_row = _nbr_windows(rows)
    qc = jnp.arange(W)[:, None]
    kc = jnp.arange(W)[None, :]
    c_start = jnp.clip(qc - kw // 2, 0, W - kw)
    inside_col = (kc >= c_start) & (kc < c_start + kw)
    col_idx = jnp.clip(kc - qc + (kw - 1), 0, 2 * kw - 2)
    row_sel = jax.nn.one_hot(jnp.asarray(row_idx, jnp.int32), 2 * C_WIN_H - 1, dtype=F32)
    col_sel = jax.nn.one_hot(col_idx, 2 * kw - 1, dtype=F32)
    b = jnp.einsum('hrc,vair->hvaic', rpb, row_sel, precision=lax.Precision.HIGHEST)
    b = jnp.einsum('hvaic,qkc->hvaiqk', b, col_sel, precision=lax.Precision.HIGHEST)
    ok = jnp.asarray(inside_row)[None, :, :, :, None, None] & inside_col[None, None, None, None]
    b = jnp.where(ok, b, NEG).transpose(1, 0, 3, 5, 2, 4)
    nv = len(row_idx)
    return b.reshape(nv, C_HEADS, 2 * NBR_KBLK * W, NBR_Q).astype(F32), jnp.asarray(var_of_u, jnp.int32)


def _nbr_attention(q, k, vt, bias, var_of_u):
    B, H, S, d = q.shape
    nu = S // NBR_Q

    def first(u):
        return jnp.clip(u - C_WIN_H // 4, 0, nu - NBR_KBLK)

    k_specs = [pl.BlockSpec((1, H, NBR_Q, d), functools.partial(lambda b, u, var, i: (b, 0, first(u) + i, 0), i=i))
               for i in range(NBR_KBLK)]
    vt_specs = [pl.BlockSpec((1, H, 1, V_ROWS, NBR_Q), functools.partial(lambda b, u, var, i: (b, 0, first(u) + i, 0, 0), i=i))
                for i in range(NBR_KBLK)]
    grid_spec = pltpu.PrefetchScalarGridSpec(
        num_scalar_prefetch=1,
        grid=(B, nu),
        in_specs=[pl.BlockSpec((1, H, NBR_Q, d), lambda b, u, var: (b, 0, u, 0))] + k_specs + vt_specs
        + [pl.BlockSpec((1, H, NBR_KBLK * NBR_Q, NBR_Q), lambda b, u, var: (var[u], 0, 0, 0))],
        out_specs=pl.BlockSpec((1, NBR_Q, H * d), lambda b, u, var: (b, u, 0)),
    )
    return pl.pallas_call(
        _nbr_kernel,
        grid_spec=grid_spec,
        out_shape=jax.ShapeDtypeStruct((B, S, H * d), BF16),
        compiler_params=_params("parallel", "arbitrary"),
        name="nbr_attention",
    )(var_of_u, q, *([k] * NBR_KBLK), *([vt] * NBR_KBLK), bias)


def _out_ln_kernel(a_ref, w_ref, x_ref, g_ref, b_ref, o_ref):
    h = jnp.dot(a_ref[...], w_ref[...], preferred_element_type=F32)
    o_ref[...] = _layer_norm(DEEPNORM_ALPHA * x_ref[...] + h, g_ref[...], b_ref[...])


def _out_ln(a, w, x, g, b, bm):
    N = x.shape[0]
    row = lambda i: (i, 0)
    return pl.pallas_call(
        _out_ln_kernel,
        grid=(N // bm,),
        in_specs=[pl.BlockSpec((bm, a.shape[1]), row), _full(w.shape), pl.BlockSpec((bm, D_MODEL), row),
                  _full(g.shape), _full(b.shape)],
        out_specs=pl.BlockSpec((bm, D_MODEL), row),
        out_shape=jax.ShapeDtypeStruct((N, D_MODEL), F32),
        compiler_params=_params("parallel"),
        name="out_proj_ln",
    )(a, w, x, g, b)


def _swiglu_up_kernel(x_ref, wg_ref, wu_ref, h_ref):
    xb = x_ref[...].astype(BF16)
    g = jnp.dot(xb, wg_ref[...], preferred_element_type=F32)
    u = jnp.dot(xb, wu_ref[...], preferred_element_type=F32)
    h_ref[...] = (g * jax.nn.sigmoid(g) * u).astype(BF16)


def _swiglu_up(x, wg, wu, bm, bn):
    N = x.shape[0]
    F = wg.shape[1]
    return pl.pallas_call(
        _swiglu_up_kernel,
        grid=(F // bn, N // bm),
        in_specs=[pl.BlockSpec((bm, D_MODEL), lambda j, i: (i, 0)), pl.BlockSpec((D_MODEL, bn), lambda j, i: (0, j)),
                  pl.BlockSpec((D_MODEL, bn), lambda j, i: (0, j))],
        out_specs=pl.BlockSpec((bm, bn), lambda j, i: (i, j)),
        out_shape=jax.ShapeDtypeStruct((N, F), BF16),
        compiler_params=_params("parallel", "parallel"),
        name="swiglu_up",
    )(x, wg, wu)


def _ple(x, p_ref, wpg_ref, wpi_ref):
    gate = jax.nn.sigmoid(jnp.dot(x.astype(BF16), wpg_ref[...], preferred_element_type=F32))
    return gate * jnp.dot(p_ref[0].astype(BF16), wpi_ref[...], preferred_element_type=F32)


def _ffn_down_ln_kernel(h_ref, wd_ref, x_ref, p_ref, wpg_ref, wpi_ref, g_ref, b_ref, o_ref):
    x = x_ref[...]
    f = jnp.dot(h_ref[...], wd_ref[...], preferred_element_type=F32)
    o_ref[...] = _layer_norm(DEEPNORM_ALPHA * x + f + _ple(x, p_ref, wpg_ref, wpi_ref), g_ref[...], b_ref[...])


def _ffn_down_ln(h, wd, x, p, layer, wpg, wpi, g, b, bm):
    N = x.shape[0]
    row = lambda i: (i, 0)
    return pl.pallas_call(
        _ffn_down_ln_kernel,
        grid=(N // bm,),
        in_specs=[pl.BlockSpec((bm, h.shape[1]), row), _full(wd.shape), pl.BlockSpec((bm, D_MODEL), row),
                  pl.BlockSpec((1, bm, PLE_DIM), lambda i: (layer, i, 0)), _full(wpg.shape), _full(wpi.shape),
                  _full(g.shape), _full(b.shape)],
        out_specs=pl.BlockSpec((bm, D_MODEL), row),
        out_shape=jax.ShapeDtypeStruct((N, D_MODEL), F32),
        compiler_params=_params("parallel"),
        name="ffn_down_ple_ln",
    )(h, wd, x, p, wpg, wpi, g, b)


def _router_kernel(x_ref, w_ref, idx_ref, gate_ref, xb_ref):
    x = x_ref[...]
    xb_ref[...] = x.astype(BF16)
    logits = jnp.dot(x, w_ref[...], preferred_element_type=F32, precision=lax.Precision.HIGHEST)
    lane = lax.broadcasted_iota(jnp.int32, logits.shape, 1).astype(F32)
    logits = jnp.where(lane < N_EXPERTS, logits, NEG)
    m1 = jnp.max(logits, axis=-1, keepdims=True)
    i1 = jnp.min(jnp.where(logits == m1, lane, 128.0), axis=-1, keepdims=True)
    rest = jnp.where(lane == i1, NEG, logits)
    m2 = jnp.max(rest, axis=-1, keepdims=True)
    i2 = jnp.min(jnp.where(rest == m2, lane, 128.0), axis=-1, keepdims=True)
    e = jnp.exp(m2 - m1)
    g1 = 1.0 / (1.0 + e)
    idx_ref[...] = jnp.where(lane == 0.0, i1, jnp.where(lane == 1.0, i2, 0.0)).astype(jnp.int32)
    gate_ref[...] = jnp.where(lane == 0.0, g1, jnp.where(lane == 1.0, e * g1, 0.0))


def _router(x, w, bm):
    N = x.shape[0]
    row = lambda i: (i, 0)
    return pl.pallas_call(
        _router_kernel,
        grid=(N // bm,),
        in_specs=[pl.BlockSpec((bm, D_MODEL), row), _full(w.shape)],
        out_specs=[pl.BlockSpec((bm, 128), row)] * 2 + [pl.BlockSpec((bm, D_MODEL), row)],
        out_shape=[jax.ShapeDtypeStruct((N, 128), jnp.int32), jax.ShapeDtypeStruct((N, 128), F32),
                   jax.ShapeDtypeStruct((N, D_MODEL), BF16)],
        compiler_params=_params("parallel"),
        name="moe_router",
    )(x, w)


def _moe_ffn_kernel(te_ref, nt_ref, xs_ref, wg_ref, wu_ref, wd_ref, o_ref, acc_ref):
    t = pl.program_id(0)
    j = pl.program_id(1)

    @pl.when(t < nt_ref[0])
    def _():
        xs = xs_ref[...]
        g = jnp.dot(xs, wg_ref[0, 0], preferred_element_type=F32)
        u = jnp.dot(xs, wu_ref[0, 0], preferred_element_type=F32)
        h = (g * jax.nn.sigmoid(g) * u).astype(BF16)
        y = jnp.dot(h, wd_ref[0, 0], preferred_element_type=F32)

        @pl.when(j == 0)
        def _():
            acc_ref[...] = y

        @pl.when(j == pl.num_programs(1) - 1)
        def _():
            o_ref[...] = (acc_ref[...] + y).astype(o_ref.dtype)

    @pl.when(t >= nt_ref[0])
    def _():
        o_ref[...] = jnp.zeros_like(o_ref)


def _moe_ffn(tile_expert, n_tiles, xs, wg, wu, wd, layer, tm, halves=2):
    P = xs.shape[0]
    fh = FF_EXPERT // halves
    grid_spec = pltpu.PrefetchScalarGridSpec(
        num_scalar_prefetch=2,
        grid=(P // tm, halves),
        in_specs=[pl.BlockSpec((tm, D_MODEL), lambda t, j, te, nt: (t, 0)),
                  pl.BlockSpec((1, 1, D_MODEL, fh), lambda t, j, te, nt: (layer, te[t], 0, j)),
                  pl.BlockSpec((1, 1, D_MODEL, fh), lambda t, j, te, nt: (layer, te[t], 0, j)),
                  pl.BlockSpec((1, 1, fh, D_MODEL), lambda t, j, te, nt: (layer, te[t], j, 0))],
        out_specs=pl.BlockSpec((tm, D_MODEL), lambda t, j, te, nt: (t, 0)),
        scratch_shapes=[pltpu.VMEM((tm, D_MODEL), F32)],
    )
    return pl.pallas_call(
        _moe_ffn_kernel,
        grid_spec=grid_spec,
        out_shape=jax.ShapeDtypeStruct((P, D_MODEL), BF16),
        compiler_params=_params("arbitrary", "arbitrary"),
        name="moe_expert_ffn",
    )(tile_expert, n_tiles, xs, wg, wu, wd)


def _moe_combine_ln_kernel(ya_ref, yb_ref, gate_ref, x_ref, p_ref, wpg_ref, wpi_ref, g_ref, b_ref, o_ref):
    x = x_ref[...]
    gate = gate_ref[...]
    f = gate[:, 0:1] * ya_ref[...] + gate[:, 1:2] * yb_ref[...]
    o_ref[...] = _layer_norm(DEEPNORM_ALPHA * x + f + _ple(x, p_ref, wpg_ref, wpi_ref), g_ref[...], b_ref[...])


def _moe_combine_ln(ya, yb, gates, x, p, layer, wpg, wpi, g, b, bm):
    N = x.shape[0]
    row = lambda i: (i, 0)
    rows = pl.BlockSpec((bm, D_MODEL), row)
    return pl.pallas_call(
        _moe_combine_ln_kernel,
        grid=(N // bm,),
        in_specs=[rows, rows, pl.BlockSpec((bm, 128), row), rows, pl.BlockSpec((1, bm, PLE_DIM), lambda i: (layer, i, 0)),
                  _full(wpg.shape), _full(wpi.shape), _full(g.shape), _full(b.shape)],
        out_specs=rows,
        out_shape=jax.ShapeDtypeStruct((N, D_MODEL), F32),
        compiler_params=_params("parallel"),
        name="moe_combine_ple_ln",
    )(ya, yb, gates, x, p, wpg, wpi, g, b)


def _cast_kernel(x_ref, o_ref):
    o_ref[...] = x_ref[...].astype(o_ref.dtype)


def _to_bf16(w):
    cols = w.shape[-1]
    w2 = w.reshape(-1, cols)
    br = CAST_BLOCK_ELEMS // cols
    assert w2.shape[0] % br == 0 and br % 16 == 0
    spec = pl.BlockSpec((br, cols), lambda i: (i, 0))
    out = pl.pallas_call(
        _cast_kernel,
        grid=(w2.shape[0] // br,),
        in_specs=[spec],
        out_specs=spec,
        out_shape=jax.ShapeDtypeStruct(w2.shape, BF16),
        compiler_params=_params("parallel"),
        name="cast_bf16",
    )(w2)
    return out.reshape(w.shape)


def _route(idx, tm):
    N = idx.shape[0]
    e_flat = idx.reshape(-1)
    onehot = (e_flat[:, None] == jnp.arange(N_EXPERTS, dtype=jnp.int32)[None, :]).astype(jnp.int32)
    csum = jnp.cumsum(onehot, axis=0)
    rank = jnp.sum(csum * onehot, axis=1) - 1
    counts = csum[-1]
    padded = ((counts + tm - 1) // tm) * tm
    ends = jnp.cumsum(padded)
    slot = (ends - padded)[e_flat] + rank
    P = TOP_K * N + N_EXPERTS * tm
    row_token = jnp.zeros((P,), jnp.int32).at[slot].set(
        jnp.arange(TOP_K * N, dtype=jnp.int32) // TOP_K, unique_indices=True, mode="promise_in_bounds")
    tile_start = jnp.arange(P // tm, dtype=jnp.int32) * tm
    tile_expert = jnp.minimum(jnp.sum((tile_start[:, None] >= ends[None, :]).astype(jnp.int32), axis=1), N_EXPERTS - 1)
    n_tiles = (ends[-1] // tm).astype(jnp.int32).reshape(1)
    return slot.reshape(N, TOP_K), row_token, tile_expert.astype(jnp.int32), n_tiles


def _swap_halves(w, group):
    shp = w.shape
    w = w.reshape(shp[:-1] + (shp[-1] // group, 2, group // 2))
    return jnp.flip(w, axis=-2).reshape(shp)


def _rope_tables(pos, dim):
    inv = ROPE_THETA ** (-jnp.arange(0, dim, 2, dtype=F32) / dim)
    ang = pos.astype(F32)[:, None] * inv[None, :]
    c, s = jnp.cos(ang), jnp.sin(ang)
    return jnp.concatenate([c, c], -1), jnp.concatenate([-s, s], -1)


def _mla_prep(w_dq, w_dkv, w_uq, w_ukv, S):
    wd = jnp.zeros((D_MODEL, 896), F32)
    wd = wd.at[:, :A_Q_LORA].set(w_dq).at[:, A_Q_LORA:640].set(w_dkv[:, :A_KV_LORA])
    w_r = w_dkv[:, A_KV_LORA:]
    wd = wd.at[:, 640 + A_NOPE:640 + A_NOPE + A_ROPE].set(w_r)
    wd = wd.at[:, 768 + A_NOPE:768 + A_NOPE + A_ROPE].set(_swap_halves(w_r, A_ROPE))
    uq = w_uq.reshape(A_Q_LORA, A_HEADS, A_NOPE + A_ROPE)
    w1 = jnp.pad(uq, ((0, 0), (0, 0), (0, A_DK - A_NOPE - A_ROPE))).reshape(A_Q_LORA, A_HEADS * A_DK)
    w2 = jnp.pad(_swap_halves(uq[..., A_NOPE:], A_ROPE), ((0, 0), (0, 0), (A_NOPE, A_DK - A_NOPE - A_ROPE)))
    w2 = w2.reshape(A_Q_LORA, A_HEADS * A_DK)
    ukv = w_ukv.reshape(A_KV_LORA, A_HEADS, A_NOPE + A_V)
    wk = jnp.pad(ukv[..., :A_NOPE], ((0, 0), (0, 0), (0, A_DK - A_NOPE))).reshape(A_KV_LORA, A_HEADS * A_DK)
    wvt = ukv[..., A_NOPE:].reshape(A_KV_LORA, A_HEADS * A_V).T
    c, s = _rope_tables(jnp.arange(S), A_ROPE)
    pad = lambda t, fill: jnp.concatenate([jnp.full((S, A_NOPE), fill, F32), t, jnp.zeros((S, A_DK - A_NOPE - A_ROPE), F32)], -1)
    scale = (A_NOPE + A_ROPE) ** -0.5 * LOG2E
    tabs = dict(k_cos=pad(c, 0.0), k_sin=pad(s, 0.0), q_cos=pad(c, 1.0) * scale, q_sin=pad(s, 0.0) * scale)
    return wd.astype(BF16), w1.astype(BF16), w2.astype(BF16), wk.astype(BF16), wvt.astype(BF16), tabs


def _gqa_prep(w_qkv, g_q, g_k, S):
    nq = B_Q_HEADS * B_HEAD_DIM
    nk = B_KV_HEADS * B_HEAD_DIM
    half = B_HEAD_DIM // 2
    wq, wk, wv = w_qkv[:, :nq], w_qkv[:, nq:nq + nk], w_qkv[:, nq + nk:]
    wkk = jnp.concatenate([wk, _swap_halves(wk, half)], axis=1)
    t = jnp.arange(S)
    cr, sr = _rope_tables(t // GRID_W, half)
    cc, sc = _rope_tables(t % GRID_W, half)
    cos_t = jnp.tile(jnp.concatenate([cr, cc], -1), (1, 2))
    sin_t = jnp.tile(jnp.concatenate([sr, sc], -1), (1, 2))
    lane = jnp.arange(nq) // B_HEAD_DIM
    ones_bd = (lane[:, None] == lane[None, :]).astype(BF16)
    gq = jnp.tile(g_q, B_Q_HEADS)[None, :]
    gk = jnp.tile(g_k, B_KV_HEADS)[None, :]
    return (wq.astype(BF16), _swap_halves(wq, half).astype(BF16), wkk.astype(BF16), wv.T.astype(BF16), ones_bd,
            gq, _swap_halves(gq, half), gk, _swap_halves(gk, half), cos_t, sin_t)


def _mixer(i, x, B, S, w, bm):
    kind, j = i % N_MIXERS, i // N_MIXERS
    if kind == 0:
        wd, w1, w2, wk, wvt, tabs = _mla_prep(w['a_w_dq'][j], w['a_w_dkv'][j], w['a_w_uq'][j], w['a_w_ukv'][j], S)
        cq, ckv, kr = _mla_down(x, wd, w['a_g_q'][j][None, :], w['a_g_kv'][j][None, :], tabs['k_cos'], tabs['k_sin'], S, bm)
        q = _mla_q_up(cq, w1, w2, tabs['q_cos'], tabs['q_sin'], B, S, bm)
        k, vt = _mla_kv_up(ckv, kr, wk, wvt, B, S, min(KV_CHUNK, S))
        a = _flash(q, k, vt, hp=2, G=1, tq=min(1024, S), nbuf=2)
        w_o = w['a_w_o'][j]
    elif kind == 1:
        prep = _gqa_prep(w['b_w_qkv'][j], w['b_g_q'][j], w['b_g_k'][j], S)
        q, k, vt = _gqa_proj(x, *prep, B, S, bm)
        a = _flash(q, k, vt, hp=1, G=B_Q_HEADS // B_KV_HEADS, tq=min(256, S), nbuf=2)
        w_o = w['b_w_o'][j]
    else:
        n = C_HEADS * C_HEAD_DIM
        w_qkv = w['c_w_qkv'][j]
        q, k, vt = _nbr_proj(x, w_qkv[:, :2 * n].astype(BF16), w_qkv[:, 2 * n:].T.astype(BF16), B, S, bm)
        a = _nbr_attention(q, k, vt, *_nbr_bias(w['c_rpb'][j], S // GRID_W))
        w_o = w['c_w_o'][j]
    return a.reshape(B * S, D_MODEL), w_o.astype(BF16)


def _trunk(groups, w):
    shapes = [x.shape[:2] for x, _ in groups]
    xs = [x.reshape(B * S, D_MODEL) for (x, _), (B, S) in zip(groups, shapes)]
    ps = [p.reshape(p.shape[0], B * S, PLE_DIM) for (_, p), (B, S) in zip(groups, shapes)]
    bms = [min(ROW_BLOCK, S) for _, S in shapes]
    moe_w = {k: _to_bf16(w[k]) for k in ('m_w_gate', 'm_w_up', 'm_w_down')}
    for i in range(DEPTH):
        for g, (B, S) in enumerate(shapes):
            a, w_o = _mixer(i, xs[g], B, S, w, bms[g])
            xs[g] = _out_ln(a, w_o, xs[g], w['ln1_g'][i][None, :], w['ln1_b'][i][None, :], bms[g])
        f_i = i // 2
        wpg, wpi = w['ple_w_gate'][i].astype(BF16), w['ple_w_in'][i].astype(BF16)
        g2, b2 = w['ln2_g'][i][None, :], w['ln2_b'][i][None, :]
        if i % 2 == 0:
            wg, wu, wd = (w[k][f_i].astype(BF16) for k in ('f_w_gate', 'f_w_up', 'f_w_down'))
            for g in range(len(xs)):
                h = _swiglu_up(xs[g], wg, wu, bms[g], FF_DENSE // 2)
                xs[g] = _ffn_down_ln(h, wd, xs[g], ps[g], i, wpg, wpi, g2, b2, bms[g])
        else:
            w_r = jnp.pad(w['m_w_router'][f_i], ((0, 0), (0, 128 - N_EXPERTS)))
            routed = [_router(x, w_r, bm) for x, bm in zip(xs, bms)]
            idx = jnp.concatenate([r[0][:, :TOP_K] for r in routed], axis=0)
            xb = jnp.concatenate([r[2] for r in routed], axis=0)
            slot, row_token, tile_expert, n_tiles = _route(idx, MOE_TILE)
            rows = xb.at[row_token].get(mode="promise_in_bounds")
            ys = _moe_ffn(tile_expert, n_tiles, rows, moe_w['m_w_gate'], moe_w['m_w_up'], moe_w['m_w_down'], f_i, MOE_TILE)
            off = 0
            for g in range(len(xs)):
                n = xs[g].shape[0]
                ya = ys.at[slot[off:off + n, 0]].get(mode="promise_in_bounds")
                yb = ys.at[slot[off:off + n, 1]].get(mode="promise_in_bounds")
                xs[g] = _moe_combine_ln(ya, yb, routed[g][1], xs[g], ps[g], i, wpg, wpi, g2, b2, bms[g])
                off += n
    return tuple(x.reshape(B, S, D_MODEL) for x, (B, S) in zip(xs, shapes))


def kernel(x_prompt, x_sample, p_prompt, p_sample, a_w_dq, a_g_q, a_w_uq, a_w_dkv, a_g_kv, a_w_ukv, a_w_o, b_w_qkv, b_g_q, b_g_k, b_w_o, c_w_qkv, c_rpb, c_w_o, ln1_g, ln1_b, ln2_g, ln2_b, f_w_gate, f_w_up, f_w_down, m_w_router, m_w_gate, m_w_up, m_w_down, ple_w_gate, ple_w_in):
    w = dict(a_w_dq=a_w_dq, a_g_q=a_g_q, a_w_uq=a_w_uq, a_w_dkv=a_w_dkv, a_g_kv=a_g_kv, a_w_ukv=a_w_ukv, a_w_o=a_w_o,
             b_w_qkv=b_w_qkv, b_g_q=b_g_q, b_g_k=b_g_k, b_w_o=b_w_o, c_w_qkv=c_w_qkv, c_rpb=c_rpb, c_w_o=c_w_o,
             ln1_g=ln1_g, ln1_b=ln1_b, ln2_g=ln2_g, ln2_b=ln2_b, f_w_gate=f_w_gate, f_w_up=f_w_up, f_w_down=f_w_down,
             m_w_router=m_w_router, m_w_gate=m_w_gate, m_w_up=m_w_up, m_w_down=m_w_down,
             ple_w_gate=ple_w_gate, ple_w_in=ple_w_in)
    return _trunk([(x_prompt, p_prompt), (x_sample, p_sample)], w)
```

```python
import functools

import jax
import jax.numpy as jnp
from jax import lax
from jax.experimental import pallas as pl
from jax.experimental.pallas import tpu as pltpu

F32 = jnp.float32
BF16 = jnp.bfloat16

D_MODEL = 1024
DEPTH = 4
GRID_W = 64
PLE_DIM = 256
N_MIXERS = 3
ROPE_THETA = 10000.0
RMS_EPS = 1e-6
LN_EPS = 1e-5
DEEPNORM_ALPHA = (2 * DEPTH) ** 0.25

A_HEADS = 16
A_Q_LORA = 384
A_KV_LORA = 256
A_NOPE = 64
A_ROPE = 32
A_V = 64
A_DK = 128
B_Q_HEADS = 16
B_KV_HEADS = 4
B_HEAD_DIM = 64
C_HEADS = 16
C_HEAD_DIM = 64
C_WIN_H = 8
C_WIN_W = 16
FF_DENSE = 2816
N_EXPERTS = 8
TOP_K = 2
FF_EXPERT = 3584

V_ROWS = 80
NEG = -1e30
LOG2E = 1.4426950408889634
VMEM_LIMIT = 56 * 1024 * 1024
ROW_BLOCK = 512
KV_CHUNK = 512
PV_STRIP = 256
MOE_TILE = 512
CAST_BLOCK_ELEMS = 7 * 2 ** 18
NBR_Q = 2 * GRID_W
NBR_KBLK = (C_WIN_H + 2) // 2
NT = (((1,), (1,)), ((), ()))


def _params(*sem):
    return pltpu.CompilerParams(dimension_semantics=sem, vmem_limit_bytes=VMEM_LIMIT)


def _full(shape):
    return pl.BlockSpec(shape, lambda *_: (0,) * len(shape))


def _rms(x, g):
    return x * lax.rsqrt(jnp.mean(x * x, axis=-1, keepdims=True) + RMS_EPS) * g


def _layer_norm(x, g, b):
    mu = jnp.mean(x, axis=-1, keepdims=True)
    xc = x - mu
    var = jnp.mean(xc * xc, axis=-1, keepdims=True)
    return xc * lax.rsqrt(var + LN_EPS) * g + b


def _tile_lanes(t, n):
    return jnp.concatenate([t] * n, axis=-1)


def _mla_down_kernel(x_ref, w_ref, gq_ref, gkv_ref, cos_ref, sin_ref, cq_ref, ckv_ref, kr_ref):
    acc = jnp.dot(x_ref[...].astype(BF16), w_ref[...], preferred_element_type=F32)
    cq_ref[...] = _rms(acc[:, :A_Q_LORA], gq_ref[...]).astype(BF16)
    ckv_ref[...] = _rms(acc[:, A_Q_LORA:640], gkv_ref[...]).astype(BF16)
    kr_ref[...] = (acc[:, 640:768] * cos_ref[...] + acc[:, 768:896] * sin_ref[...]).astype(BF16)


def _mla_down(x, w, gq, gkv, cos_t, sin_t, S, bm):
    N = x.shape[0]
    nsb = S // bm
    row = lambda i: (i, 0)
    tab = lambda i: (i % nsb, 0)
    return pl.pallas_call(
        _mla_down_kernel,
        grid=(N // bm,),
        in_specs=[pl.BlockSpec((bm, D_MODEL), row), _full(w.shape), _full(gq.shape), _full(gkv.shape),
                  pl.BlockSpec((bm, 128), tab), pl.BlockSpec((bm, 128), tab)],
        out_specs=[pl.BlockSpec((bm, A_Q_LORA), row), pl.BlockSpec((bm, A_KV_LORA), row),
                   pl.BlockSpec((bm, 128), row)],
        out_shape=[jax.ShapeDtypeStruct((N, A_Q_LORA), BF16), jax.ShapeDtypeStruct((N, A_KV_LORA), BF16),
                   jax.ShapeDtypeStruct((N, 128), BF16)],
        compiler_params=_params("parallel"),
        name="mla_down",
    )(x, w, gq, gkv, cos_t, sin_t)


def _mla_q_up_kernel(cq_ref, w1_ref, w2_ref, cos_ref, sin_ref, q_ref):
    cq = cq_ref[...]
    a1 = jnp.dot(cq, w1_ref[...], preferred_element_type=F32)
    a2 = jnp.dot(cq, w2_ref[...], preferred_element_type=F32)
    q = a1 * _tile_lanes(cos_ref[...], A_HEADS) + a2 * _tile_lanes(sin_ref[...], A_HEADS)
    for h in range(A_HEADS):
        q_ref[0, h] = q[:, h * A_DK:(h + 1) * A_DK].astype(BF16)


def _mla_q_up(cq, w1, w2, cos_t, sin_t, B, S, bm):
    nsb = S // bm
    return pl.pallas_call(
        _mla_q_up_kernel,
        grid=(B * nsb,),
        in_specs=[pl.BlockSpec((bm, A_Q_LORA), lambda i: (i, 0)), _full(w1.shape), _full(w2.shape),
                  pl.BlockSpec((bm, 128), lambda i: (i % nsb, 0)), pl.BlockSpec((bm, 128), lambda i: (i % nsb, 0))],
        out_specs=pl.BlockSpec((1, A_HEADS, bm, A_DK), lambda i: (i // nsb, 0, i % nsb, 0)),
        out_shape=jax.ShapeDtypeStruct((B, A_HEADS, S, A_DK), BF16),
        compiler_params=_params("parallel"),
        name="mla_q_up",
    )(cq, w1, w2, cos_t, sin_t)


def _ones_rows(n):
    r = lax.broadcasted_iota(jnp.int32, (V_ROWS - 64, n), 0)
    return jnp.where(r == 0, 1.0, 0.0).astype(BF16)


def _mla_kv_up_kernel(ckv_ref, kr_ref, wk_ref, wvt_ref, k_ref, vt_ref):
    ckv = ckv_ref[...]
    ak = jnp.dot(ckv, wk_ref[...], preferred_element_type=F32)
    kr = kr_ref[...].astype(F32)
    avt = lax.dot_general(wvt_ref[...], ckv, NT, preferred_element_type=F32)
    ones = _ones_rows(ckv.shape[0])
    for h in range(A_HEADS):
        k_ref[0, h] = (ak[:, h * A_DK:(h + 1) * A_DK] + kr).astype(BF16)
        vt_ref[0, h, 0, 0:A_V, :] = avt[h * A_V:(h + 1) * A_V, :].astype(BF16)
        vt_ref[0, h, 0, A_V:V_ROWS, :] = ones


def _mla_kv_up(ckv, kr, wk, wvt, B, S, bm):
    nsb = S // bm
    return pl.pallas_call(
        _mla_kv_up_kernel,
        grid=(B * nsb,),
        in_specs=[pl.BlockSpec((bm, A_KV_LORA), lambda i: (i, 0)), pl.BlockSpec((bm, 128), lambda i: (i, 0)),
                  _full(wk.shape), _full(wvt.shape)],
        out_specs=[pl.BlockSpec((1, A_HEADS, bm, A_DK), lambda i: (i // nsb, 0, i % nsb, 0)),
                   pl.BlockSpec((1, A_HEADS, 1, V_ROWS, bm), lambda i: (i // nsb, 0, i % nsb, 0, 0))],
        out_shape=[jax.ShapeDtypeStruct((B, A_HEADS, S, A_DK), BF16),
                   jax.ShapeDtypeStruct((B, A_HEADS, nsb, V_ROWS, bm), BF16)],
        compiler_params=_params("parallel"),
        name="mla_kv_up",
    )(ckv, kr, wk, wvt)


def _gqa_proj_kernel(x_ref, wq_ref, wqp_ref, wk_ref, wvt_ref, ones_ref, gq_ref, gqp_ref, gk_ref, gkp_ref,
                     cos_ref, sin_ref, q_ref, k_ref, vt_ref):
    xb = x_ref[...].astype(BF16)
    nq = B_Q_HEADS * B_HEAD_DIM
    nk = B_KV_HEADS * B_HEAD_DIM
    cos = cos_ref[...]
    sin = sin_ref[...]
    ones_bd = ones_ref[...]

    def norm_rope(a, ap, g, gp, n):
        ss = jnp.dot((a * a).astype(BF16), ones_bd[:n, :n], preferred_element_type=F32) * (1.0 / B_HEAD_DIM)
        r = lax.rsqrt(ss + RMS_EPS)
        reps = n // 128
        return r * (a * g * _tile_lanes(cos, reps) + ap * gp * _tile_lanes(sin, reps))

    aq = jnp.dot(xb, wq_ref[...], preferred_element_type=F32)
    aqp = jnp.dot(xb, wqp_ref[...], preferred_element_type=F32)
    q = norm_rope(aq, aqp, gq_ref[...], gqp_ref[...], nq) * (B_HEAD_DIM ** -0.5 * LOG2E)
    akk = jnp.dot(xb, wk_ref[...], preferred_element_type=F32)
    k = norm_rope(akk[:, :nk], akk[:, nk:], gk_ref[...], gkp_ref[...], nk)
    avt = lax.dot_general(wvt_ref[...], xb, NT, preferred_element_type=F32)
    ones = _ones_rows(xb.shape[0])
    for h in range(B_Q_HEADS):
        q_ref[0, h] = q[:, h * B_HEAD_DIM:(h + 1) * B_HEAD_DIM].astype(BF16)
    for h in range(B_KV_HEADS):
        k_ref[0, h] = k[:, h * B_HEAD_DIM:(h + 1) * B_HEAD_DIM].astype(BF16)
        vt_ref[0, h, 0, 0:B_HEAD_DIM, :] = avt[h * B_HEAD_DIM:(h + 1) * B_HEAD_DIM, :].astype(BF16)
        vt_ref[0, h, 0, B_HEAD_DIM:V_ROWS, :] = ones


def _gqa_proj(x, wq, wqp, wk, wvt, ones_bd, gq, gqp, gk, gkp, cos_t, sin_t, B, S, bm):
    nsb = S // bm
    hm = lambda i: (i // nsb, 0, i % nsb, 0)
    tab = lambda i: (i % nsb, 0)
    ins = [x, wq, wqp, wk, wvt, ones_bd, gq, gqp, gk, gkp, cos_t, sin_t]
    specs = [pl.BlockSpec((bm, D_MODEL), lambda i: (i, 0))] + [_full(a.shape) for a in ins[1:10]]
    specs += [pl.BlockSpec((bm, 128), tab), pl.BlockSpec((bm, 128), tab)]
    return pl.pallas_call(
        _gqa_proj_kernel,
        grid=(B * nsb,),
        in_specs=specs,
        out_specs=[pl.BlockSpec((1, B_Q_HEADS, bm, B_HEAD_DIM), hm), pl.BlockSpec((1, B_KV_HEADS, bm, B_HEAD_DIM), hm),
                   pl.BlockSpec((1, B_KV_HEADS, 1, V_ROWS, bm), lambda i: (i // nsb, 0, i % nsb, 0, 0))],
        out_shape=[jax.ShapeDtypeStruct((B, B_Q_HEADS, S, B_HEAD_DIM), BF16),
                   jax.ShapeDtypeStruct((B, B_KV_HEADS, S, B_HEAD_DIM), BF16),
                   jax.ShapeDtypeStruct((B, B_KV_HEADS, nsb, V_ROWS, bm), BF16)],
        compiler_params=_params("parallel"),
        name="gqa_proj",
    )(*ins)


def _flash_kernel(q_ref, k_ref, vt_ref, o_ref, *scratch, hp, G, tq, tk, nk, dv, unroll, nbuf):
    s_bufs, acc_ref = scratch[:nbuf], scratch[nbuf]
    TQ = G * tq
    dk = q_ref.shape[-1]
    outs = []
    for h in range(hp):
        q = q_ref[0, h * G:(h + 1) * G].reshape(TQ, dk)

        def scores(c, s_ref, h=h, q=q):
            off = c * tk if isinstance(c, int) else pl.multiple_of(c * tk, tk)
            s = lax.dot_general(k_ref[0, h, pl.ds(off, tk), :], q, NT, preferred_element_type=F32)
            s_ref[...] = s
            return jnp.max(s, axis=0, keepdims=True)

        def step(c, i, m, a_cur, last, h=h):
            if not last:
                cm = scores(c + 1, s_bufs[(i + 1) % nbuf])
            s_cur = s_bufs[i % nbuf]
            vt = vt_ref[0, h, c]
            strip = min(PV_STRIP, TQ)
            for lo in range(0, TQ, strip):
                sl = slice(lo, lo + strip)
                p = jnp.exp2((s_cur[:, sl] - m[:, sl]).astype(BF16))
                acc_ref[:, sl] = a_cur[:, sl] * acc_ref[:, sl] + jnp.dot(vt, p, preferred_element_type=F32)
            if last:
                return m, a_cur
            m_new = jnp.maximum(m, cm)
            return m_new, jnp.exp2(m - m_new)

        def group(jj, carry, last=False):
            m, a_cur = carry
            for i in range(unroll):
                m, a_cur = step(unroll * jj + i, i, m, a_cur, last and i == unroll - 1)
            return m, a_cur

        acc_ref[...] = jnp.zeros_like(acc_ref)
        m0 = jnp.maximum(scores(0, s_bufs[0]), NEG)
        ng = nk // unroll
        carry = (m0, jnp.ones((1, TQ), F32))
        if ng > 1:
            carry = lax.fori_loop(0, ng - 1, group, carry)
        group(ng - 1, carry, last=True)
        acc = acc_ref[...]
        o = acc[:dv] * (1.0 / acc[dv:dv + 1])
        for g in range(G):
            outs.append(o[:, g * tq:(g + 1) * tq])
    o_ref[0] = jnp.concatenate(outs, axis=0).T.astype(o_ref.dtype)


def _flash(q, k, vt, *, hp, G, tq, nbuf, dv=64):
    B, H, S, dk = q.shape
    Hk = k.shape[1]
    nk, tk = vt.shape[2], vt.shape[4]
    unroll = 8 if nk % 8 == 0 and nk > 8 else 4 if nk % 4 == 0 else 2
    nbuf = min(nbuf, unroll)
    assert nk % unroll == 0 and unroll % nbuf == 0
    TQ = G * tq
    scratch = [pltpu.VMEM((tk, TQ), F32)] * nbuf + [pltpu.VMEM((V_ROWS, TQ), F32)]
    kern = functools.partial(_flash_kernel, hp=hp, G=G, tq=tq, tk=tk, nk=nk, dv=dv, unroll=unroll, nbuf=nbuf)
    return pl.pallas_call(
        kern,
        grid=(B, Hk // hp, S // tq),
        in_specs=[pl.BlockSpec((1, hp * G, tq, dk), lambda b, h, i: (b, h, i, 0)),
                  pl.BlockSpec((1, hp, S, dk), lambda b, h, i: (b, h, 0, 0)),
                  pl.BlockSpec((1, hp, nk, V_ROWS, tk), lambda b, h, i: (b, h, 0, 0, 0))],
        out_specs=pl.BlockSpec((1, tq, hp * G * dv), lambda b, h, i: (b, i, h)),
        out_shape=jax.ShapeDtypeStruct((B, S, H * dv), BF16),
        scratch_shapes=scratch,
        compiler_params=_params("parallel", "parallel", "arbitrary"),
        name="flash_attention",
    )(q, k, vt)


def _nbr_proj_kernel(x_ref, w_ref, wvt_ref, q_ref, k_ref, vt_ref):
    xb = x_ref[...].astype(BF16)
    n = C_HEADS * C_HEAD_DIM
    acc = jnp.dot(xb, w_ref[...], preferred_element_type=F32)
    avt = lax.dot_general(wvt_ref[...], xb, NT, preferred_element_type=F32)
    ones = _ones_rows(NBR_Q)
    for h in range(C_HEADS):
        lo = h * C_HEAD_DIM
        q_ref[0, h] = (acc[:, lo:lo + C_HEAD_DIM] * (C_HEAD_DIM ** -0.5)).astype(BF16)
        k_ref[0, h] = acc[:, n + lo:n + lo + C_HEAD_DIM].astype(BF16)
        for c in range(xb.shape[0] // NBR_Q):
            vt_ref[0, h, c, 0:C_HEAD_DIM, :] = avt[lo:lo + C_HEAD_DIM, c * NBR_Q:(c + 1) * NBR_Q].astype(BF16)
            vt_ref[0, h, c, C_HEAD_DIM:V_ROWS, :] = ones


def _nbr_proj(x, w, wvt, B, S, bm):
    nsb = S // bm
    hm = lambda i: (i // nsb, 0, i % nsb, 0)
    shp = jax.ShapeDtypeStruct((B, C_HEADS, S, C_HEAD_DIM), BF16)
    return pl.pallas_call(
        _nbr_proj_kernel,
        grid=(B * nsb,),
        in_specs=[pl.BlockSpec((bm, D_MODEL), lambda i: (i, 0)), _full(w.shape), _full(wvt.shape)],
        out_specs=[pl.BlockSpec((1, C_HEADS, bm, C_HEAD_DIM), hm)] * 2
        + [pl.BlockSpec((1, C_HEADS, bm // NBR_Q, V_ROWS, NBR_Q), lambda i: (i // nsb, 0, i % nsb, 0, 0))],
        out_shape=[shp, shp, jax.ShapeDtypeStruct((B, C_HEADS, S // NBR_Q, V_ROWS, NBR_Q), BF16)],
        compiler_params=_params("parallel"),
        name="nbr_proj",
    )(x, w, wvt)


def _nbr_kernel(var_ref, q_ref, *refs):
    del var_ref
    k_refs = refs[:NBR_KBLK]
    vt_refs = refs[NBR_KBLK:2 * NBR_KBLK]
    bias_ref, o_ref = refs[2 * NBR_KBLK], refs[2 * NBR_KBLK + 1]
    k = jnp.concatenate([r[0] for r in k_refs], axis=1)
    vt = jnp.concatenate([r[0, :, 0] for r in vt_refs], axis=2)
    s = jnp.einsum('hkd,hqd->hkq', k, q_ref[0], preferred_element_type=F32) + bias_ref[0]
    m = jnp.max(s, axis=1, keepdims=True)
    p = jnp.exp(s - m).astype(BF16)
    acc = jnp.einsum('hdk,hkq->hdq', vt, p, preferred_element_type=F32)
    o = acc[:, :C_HEAD_DIM] * (1.0 / acc[:, C_HEAD_DIM:C_HEAD_DIM + 1])
    o_ref[0] = o.reshape(C_HEADS * C_HEAD_DIM, NBR_Q).T.astype(o_ref.dtype)


def _nbr_windows(rows):
    nkr = 2 * NBR_KBLK
    sigs, var_of_u = [], []
    for u in range(rows // 2):
        ks = min(max(2 * u - C_WIN_H // 2, 0), rows - nkr)
        sig = []
        for a in range(2):
            r = 2 * u + a
            rs = min(max(r - C_WIN_H // 2, 0), rows - C_WIN_H)
            sig.append((ks - r, ks - rs))
        sig = tuple(sig)
        if sig not in sigs:
            sigs.append(sig)
        var_of_u.append(sigs.index(sig))
    row_idx = [[[min(max(i + dr + C_WIN_H - 1, 0), 2 * C_WIN_H - 2) for i in range(nkr)] for dr, _ in sig] for sig in sigs]
    inside = [[[0 <= i + ds < C_WIN_H for i in range(nkr)] for _, ds in sig] for sig in sigs]
    return var_of_u, row_idx, inside


def _nbr_bias(rpb, rows):
    kw, W = C_WIN_W, GRID_W
    var_of_u, row_idx, inside_row = _nbr_windows(rows)
    qc = jnp.arange(W)[:, None]
    kc = jnp.arange(W)[None, :]
    c_start = jnp.clip(qc - kw // 2, 0, W - kw)
    inside_col = (kc >= c_start) & (kc < c_start + kw)
    col_idx = jnp.clip(kc - qc + (kw - 1), 0, 2 * kw - 2)
    row_sel = jax.nn.one_hot(jnp.asarray(row_idx, jnp.int32), 2 * C_WIN_H - 1, dtype=F32)
    col_sel = jax.nn.one_hot(col_idx, 2 * kw - 1, dtype=F32)
    b = jnp.einsum('hrc,vair->hvaic', rpb, row_sel, precision=lax.Precision.HIGHEST)
    b = jnp.einsum('hvaic,qkc->hvaiqk', b, col_sel, precision=lax.Precision.HIGHEST)
    ok = jnp.asarray(inside_row)[None, :, :, :, None, None] & inside_col[None, None, None, None]
    b = jnp.where(ok, b, NEG).transpose(1, 0, 3, 5, 2, 4)
    nv = len(row_idx)
    return b.reshape(nv, C_HEADS, 2 * NBR_KBLK * W, NBR_Q).astype(F32), jnp.asarray(var_of_u, jnp.int32)


def _nbr_attention(q, k, vt, bias, var_of_u):
    B, H, S, d = q.shape
    nu = S // NBR_Q

    def first(u):
        return jnp.clip(u - C_WIN_H // 4, 0, nu - NBR_KBLK)

    k_specs = [pl.BlockSpec((1, H, NBR_Q, d), functools.partial(lambda b, u, var, i: (b, 0, first(u) + i, 0), i=i))
               for i in range(NBR_KBLK)]
    vt_specs = [pl.BlockSpec((1, H, 1, V_ROWS, NBR_Q), functools.partial(lambda b, u, var, i: (b, 0, first(u) + i, 0, 0), i=i))
                for i in range(NBR_KBLK)]
    grid_spec = pltpu.PrefetchScalarGridSpec(
        num_scalar_prefetch=1,
        grid=(B, nu),
        in_specs=[pl.BlockSpec((1, H, NBR_Q, d), lambda b, u, var: (b, 0, u, 0))] + k_specs + vt_specs
        + [pl.BlockSpec((1, H, NBR_KBLK * NBR_Q, NBR_Q), lambda b, u, var: (var[u], 0, 0, 0))],
        out_specs=pl.BlockSpec((1, NBR_Q, H * d), lambda b, u, var: (b, u, 0)),
    )
    return pl.pallas_call(
        _nbr_kernel,
        grid_spec=grid_spec,
        out_shape=jax.ShapeDtypeStruct((B, S, H * d), BF16),
        compiler_params=_params("parallel", "arbitrary"),
        name="nbr_attention",
    )(var_of_u, q, *([k] * NBR_KBLK), *([vt] * NBR_KBLK), bias)


def _out_ln_kernel(a_ref, w_ref, x_ref, g_ref, b_ref, o_ref):
    h = jnp.dot(a_ref[...], w_ref[...], preferred_element_type=F32)
    o_ref[...] = _layer_norm(DEEPNORM_ALPHA * x_ref[...] + h, g_ref[...], b_ref[...])


def _out_ln(a, w, x, g, b, bm):
    N = x.shape[0]
    row = lambda i: (i, 0)
    return pl.pallas_call(
        _out_ln_kernel,
        grid=(N // bm,),
        in_specs=[pl.BlockSpec((bm, a.shape[1]), row), _full(w.shape), pl.BlockSpec((bm, D_MODEL), row),
                  _full(g.shape), _full(b.shape)],
        out_specs=pl.BlockSpec((bm, D_MODEL), row),
        out_shape=jax.ShapeDtypeStruct((N, D_MODEL), F32),
        compiler_params=_params("parallel"),
        name="out_proj_ln",
    )(a, w, x, g, b)


def _swiglu_up_kernel(x_ref, wg_ref, wu_ref, h_ref):
    xb = x_ref[...].astype(BF16)
    g = jnp.dot(xb, wg_ref[...], preferred_element_type=F32)
    u = jnp.dot(xb, wu_ref[...], preferred_element_type=F32)
    h_ref[...] = (g * jax.nn.sigmoid(g) * u).astype(BF16)


def _swiglu_up(x, wg, wu, bm, bn):
    N = x.shape[0]
    F = wg.shape[1]
    return pl.pallas_call(
        _swiglu_up_kernel,
        grid=(F // bn, N // bm),
        in_specs=[pl.BlockSpec((bm, D_MODEL), lambda j, i: (i, 0)), pl.BlockSpec((D_MODEL, bn), lambda j, i: (0, j)),
                  pl.BlockSpec((D_MODEL, bn), lambda j, i: (0, j))],
        out_specs=pl.BlockSpec((bm, bn), lambda j, i: (i, j)),
        out_shape=jax.ShapeDtypeStruct((N, F), BF16),
        compiler_params=_params("parallel", "parallel"),
        name="swiglu_up",
    )(x, wg, wu)


def _ple(x, p_ref, wpg_ref, wpi_ref):
    gate = jax.nn.sigmoid(jnp.dot(x.astype(BF16), wpg_ref[...], preferred_element_type=F32))
    return gate * jnp.dot(p_ref[0].astype(BF16), wpi_ref[...], preferred_element_type=F32)


def _ffn_down_ln_kernel(h_ref, wd_ref, x_ref, p_ref, wpg_ref, wpi_ref, g_ref, b_ref, o_ref):
    x = x_ref[...]
    f = jnp.dot(h_ref[...], wd_ref[...], preferred_element_type=F32)
    o_ref[...] = _layer_norm(DEEPNORM_ALPHA * x + f + _ple(x, p_ref, wpg_ref, wpi_ref), g_ref[...], b_ref[...])


def _ffn_down_ln(h, wd, x, p, layer, wpg, wpi, g, b, bm):
    N = x.shape[0]
    row = lambda i: (i, 0)
    return pl.pallas_call(
        _ffn_down_ln_kernel,
        grid=(N // bm,),
        in_specs=[pl.BlockSpec((bm, h.shape[1]), row), _full(wd.shape), pl.BlockSpec((bm, D_MODEL), row),
                  pl.BlockSpec((1, bm, PLE_DIM), lambda i: (layer, i, 0)), _full(wpg.shape), _full(wpi.shape),
                  _full(g.shape), _full(b.shape)],
        out_specs=pl.BlockSpec((bm, D_MODEL), row),
        out_shape=jax.ShapeDtypeStruct((N, D_MODEL), F32),
        compiler_params=_params("parallel"),
        name="ffn_down_ple_ln",
    )(h, wd, x, p, wpg, wpi, g, b)


def _router_kernel(x_ref, w_ref, idx_ref, gate_ref, xb_ref):
    x = x_ref[...]
    xb_ref[...] = x.astype(BF16)
    logits = jnp.dot(x, w_ref[...], preferred_element_type=F32, precision=lax.Precision.HIGHEST)
    lane = lax.broadcasted_iota(jnp.int32, logits.shape, 1).astype(F32)
    logits = jnp.where(lane < N_EXPERTS, logits, NEG)
    m1 = jnp.max(logits, axis=-1, keepdims=True)
    i1 = jnp.min(jnp.where(logits == m1, lane, 128.0), axis=-1, keepdims=True)
    rest = jnp.where(lane == i1, NEG, logits)
    m2 = jnp.max(rest, axis=-1, keepdims=True)
    i2 = jnp.min(jnp.where(rest == m2, lane, 128.0), axis=-1, keepdims=True)
    e = jnp.exp(m2 - m1)
    g1 = 1.0 / (1.0 + e)
    idx_ref[...] = jnp.where(lane == 0.0, i1, jnp.where(lane == 1.0, i2, 0.0)).astype(jnp.int32)
    gate_ref[...] = jnp.where(lane == 0.0, g1, jnp.where(lane == 1.0, e * g1, 0.0))


def _router(x, w, bm):
    N = x.shape[0]
    row = lambda i: (i, 0)
    return pl.pallas_call(
        _router_kernel,
        grid=(N // bm,),
        in_specs=[pl.BlockSpec((bm, D_MODEL), row), _full(w.shape)],
        out_specs=[pl.BlockSpec((bm, 128), row)] * 2 + [pl.BlockSpec((bm, D_MODEL), row)],
        out_shape=[jax.ShapeDtypeStruct((N, 128), jnp.int32), jax.ShapeDtypeStruct((N, 128), F32),
                   jax.ShapeDtypeStruct((N, D_MODEL), BF16)],
        compiler_params=_params("parallel"),
        name="moe_router",
    )(x, w)


def _moe_ffn_kernel(te_ref, nt_ref, xs_ref, wg_ref, wu_ref, wd_ref, o_ref, acc_ref):
    t = pl.program_id(0)
    j = pl.program_id(1)

    @pl.when(t < nt_ref[0])
    def _():
        xs = xs_ref[...]
        g = jnp.dot(xs, wg_ref[0, 0], preferred_element_type=F32)
        u = jnp.dot(xs, wu_ref[0, 0], preferred_element_type=F32)
        h = (g * jax.nn.sigmoid(g) * u).astype(BF16)
        y = jnp.dot(h, wd_ref[0, 0], preferred_element_type=F32)

        @pl.when(j == 0)
        def _():
            acc_ref[...] = y

        @pl.when(j == pl.num_programs(1) - 1)
        def _():
            o_ref[...] = (acc_ref[...] + y).astype(o_ref.dtype)

    @pl.when(t >= nt_ref[0])
    def _():
        o_ref[...] = jnp.zeros_like(o_ref)


def _moe_ffn(tile_expert, n_tiles, xs, wg, wu, wd, layer, tm, halves=2):
    P = xs.shape[0]
    fh = FF_EXPERT // halves
    grid_spec = pltpu.PrefetchScalarGridSpec(
        num_scalar_prefetch=2,
        grid=(P // tm, halves),
        in_specs=[pl.BlockSpec((tm, D_MODEL), lambda t, j, te, nt: (t, 0)),
                  pl.BlockSpec((1, 1, D_MODEL, fh), lambda t, j, te, nt: (layer, te[t], 0, j)),
                  pl.BlockSpec((1, 1, D_MODEL, fh), lambda t, j, te, nt: (layer, te[t], 0, j)),
                  pl.BlockSpec((1, 1, fh, D_MODEL), lambda t, j, te, nt: (layer, te[t], j, 0))],
        out_specs=pl.BlockSpec((tm, D_MODEL), lambda t, j, te, nt: (t, 0)),
        scratch_shapes=[pltpu.VMEM((tm, D_MODEL), F32)],
    )
    return pl.pallas_call(
        _moe_ffn_kernel,
        grid_spec=grid_spec,
        out_shape=jax.ShapeDtypeStruct((P, D_MODEL), BF16),
        compiler_params=_params("arbitrary", "arbitrary"),
        name="moe_expert_ffn",
    )(tile_expert, n_tiles, xs, wg, wu, wd)


def _moe_combine_ln_kernel(ya_ref, yb_ref, gate_ref, x_ref, p_ref, wpg_ref, wpi_ref, g_ref, b_ref, o_ref):
    x = x_ref[...]
    gate = gate_ref[...]
    f = gate[:, 0:1] * ya_ref[...] + gate[:, 1:2] * yb_ref[...]
    o_ref[...] = _layer_norm(DEEPNORM_ALPHA * x + f + _ple(x, p_ref, wpg_ref, wpi_ref), g_ref[...], b_ref[...])


def _moe_combine_ln(ya, yb, gates, x, p, layer, wpg, wpi, g, b, bm):
    N = x.shape[0]
    row = lambda i: (i, 0)
    rows = pl.BlockSpec((bm, D_MODEL), row)
    return pl.pallas_call(
        _moe_combine_ln_kernel,
        grid=(N // bm,),
        in_specs=[rows, rows, pl.BlockSpec((bm, 128), row), rows, pl.BlockSpec((1, bm, PLE_DIM), lambda i: (layer, i, 0)),
                  _full(wpg.shape), _full(wpi.shape), _full(g.shape), _full(b.shape)],
        out_specs=rows,
        out_shape=jax.ShapeDtypeStruct((N, D_MODEL), F32),
        compiler_params=_params("parallel"),
        name="moe_combine_ple_ln",
    )(ya, yb, gates, x, p, wpg, wpi, g, b)


def _cast_kernel(x_ref, o_ref):
    o_ref[...] = x_ref[...].astype(o_ref.dtype)


def _to_bf16(w):
    cols = w.shape[-1]
    w2 = w.reshape(-1, cols)
    br = CAST_BLOCK_ELEMS // cols
    assert w2.shape[0] % br == 0 and br % 16 == 0
    spec = pl.BlockSpec((br, cols), lambda i: (i, 0))
    out = pl.pallas_call(
        _cast_kernel,
        grid=(w2.shape[0] // br,),
        in_specs=[spec],
        out_specs=spec,
        out_shape=jax.ShapeDtypeStruct(w2.shape, BF16),
        compiler_params=_params("parallel"),
        name="cast_bf16",
    )(w2)
    return out.reshape(w.shape)


def _route(idx, tm):
    N = idx.shape[0]
    e_flat = idx.reshape(-1)
    onehot = (e_flat[:, None] == jnp.arange(N_EXPERTS, dtype=jnp.int32)[None, :]).astype(jnp.int32)
    csum = jnp.cumsum(onehot, axis=0)
    rank = jnp.sum(csum * onehot, axis=1) - 1
    counts = csum[-1]
    padded = ((counts + tm - 1) // tm) * tm
    ends = jnp.cumsum(padded)
    slot = (ends - padded)[e_flat] + rank
    P = TOP_K * N + N_EXPERTS * tm
    row_token = jnp.zeros((P,), jnp.int32).at[slot].set(
        jnp.arange(TOP_K * N, dtype=jnp.int32) // TOP_K, unique_indices=True, mode="promise_in_bounds")
    tile_start = jnp.arange(P // tm, dtype=jnp.int32) * tm
    tile_expert = jnp.minimum(jnp.sum((tile_start[:, None] >= ends[None, :]).astype(jnp.int32), axis=1), N_EXPERTS - 1)
    n_tiles = (ends[-1] // tm).astype(jnp.int32).reshape(1)
    return slot.reshape(N, TOP_K), row_token, tile_expert.astype(jnp.int32), n_tiles


def _swap_halves(w, group):
    shp = w.shape
    w = w.reshape(shp[:-1] + (shp[-1] // group, 2, group // 2))
    return jnp.flip(w, axis=-2).reshape(shp)


def _rope_tables(pos, dim):
    inv = ROPE_THETA ** (-jnp.arange(0, dim, 2, dtype=F32) / dim)
    ang = pos.astype(F32)[:, None] * inv[None, :]
    c, s = jnp.cos(ang), jnp.sin(ang)
    return jnp.concatenate([c, c], -1), jnp.concatenate([-s, s], -1)


def _mla_prep(w_dq, w_dkv, w_uq, w_ukv, S):
    wd = jnp.zeros((D_MODEL, 896), F32)
    wd = wd.at[:, :A_Q_LORA].set(w_dq).at[:, A_Q_LORA:640].set(w_dkv[:, :A_KV_LORA])
    w_r = w_dkv[:, A_KV_LORA:]
    wd = wd.at[:, 640 + A_NOPE:640 + A_NOPE + A_ROPE].set(w_r)
    wd = wd.at[:, 768 + A_NOPE:768 + A_NOPE + A_ROPE].set(_swap_halves(w_r, A_ROPE))
    uq = w_uq.reshape(A_Q_LORA, A_HEADS, A_NOPE + A_ROPE)
    w1 = jnp.pad(uq, ((0, 0), (0, 0), (0, A_DK - A_NOPE - A_ROPE))).reshape(A_Q_LORA, A_HEADS * A_DK)
    w2 = jnp.pad(_swap_halves(uq[..., A_NOPE:], A_ROPE), ((0, 0), (0, 0), (A_NOPE, A_DK - A_NOPE - A_ROPE)))
    w2 = w2.reshape(A_Q_LORA, A_HEADS * A_DK)
    ukv = w_ukv.reshape(A_KV_LORA, A_HEADS, A_NOPE + A_V)
    wk = jnp.pad(ukv[..., :A_NOPE], ((0, 0), (0, 0), (0, A_DK - A_NOPE))).reshape(A_KV_LORA, A_HEADS * A_DK)
    wvt = ukv[..., A_NOPE:].reshape(A_KV_LORA, A_HEADS * A_V).T
    c, s = _rope_tables(jnp.arange(S), A_ROPE)
    pad = lambda t, fill: jnp.concatenate([jnp.full((S, A_NOPE), fill, F32), t, jnp.zeros((S, A_DK - A_NOPE - A_ROPE), F32)], -1)
    scale = (A_NOPE + A_ROPE) ** -0.5 * LOG2E
    tabs = dict(k_cos=pad(c, 0.0), k_sin=pad(s, 0.0), q_cos=pad(c, 1.0) * scale, q_sin=pad(s, 0.0) * scale)
    return wd.astype(BF16), w1.astype(BF16), w2.astype(BF16), wk.astype(BF16), wvt.astype(BF16), tabs


def _gqa_prep(w_qkv, g_q, g_k, S):
    nq = B_Q_HEADS * B_HEAD_DIM
    nk = B_KV_HEADS * B_HEAD_DIM
    half = B_HEAD_DIM // 2
    wq, wk, wv = w_qkv[:, :nq], w_qkv[:, nq:nq + nk], w_qkv[:, nq + nk:]
    wkk = jnp.concatenate([wk, _swap_halves(wk, half)], axis=1)
    t = jnp.arange(S)
    cr, sr = _rope_tables(t // GRID_W, half)
    cc, sc = _rope_tables(t % GRID_W, half)
    cos_t = jnp.tile(jnp.concatenate([cr, cc], -1), (1, 2))
    sin_t = jnp.tile(jnp.concatenate([sr, sc], -1), (1, 2))
    lane = jnp.arange(nq) // B_HEAD_DIM
    ones_bd = (lane[:, None] == lane[None, :]).astype(BF16)
    gq = jnp.tile(g_q, B_Q_HEADS)[None, :]
    gk = jnp.tile(g_k, B_KV_HEADS)[None, :]
    return (wq.astype(BF16), _swap_halves(wq, half).astype(BF16), wkk.astype(BF16), wv.T.astype(BF16), ones_bd,
            gq, _swap_halves(gq, half), gk, _swap_halves(gk, half), cos_t, sin_t)


def _mixer(i, x, B, S, w, bm):
    kind, j = i % N_MIXERS, i // N_MIXERS
    if kind == 0:
        wd, w1, w2, wk, wvt, tabs = _mla_prep(w['a_w_dq'][j], w['a_w_dkv'][j], w['a_w_uq'][j], w['a_w_ukv'][j], S)
        cq, ckv, kr = _mla_down(x, wd, w['a_g_q'][j][None, :], w['a_g_kv'][j][None, :], tabs['k_cos'], tabs['k_sin'], S, bm)
        q = _mla_q_up(cq, w1, w2, tabs['q_cos'], tabs['q_sin'], B, S, bm)
        k, vt = _mla_kv_up(ckv, kr, wk, wvt, B, S, min(KV_CHUNK, S))
        a = _flash(q, k, vt, hp=2, G=1, tq=min(1024, S), nbuf=2)
        w_o = w['a_w_o'][j]
    elif kind == 1:
        prep = _gqa_prep(w['b_w_qkv'][j], w['b_g_q'][j], w['b_g_k'][j], S)
        q, k, vt = _gqa_proj(x, *prep, B, S, bm)
        a = _flash(q, k, vt, hp=1, G=B_Q_HEADS // B_KV_HEADS, tq=min(256, S), nbuf=2)
        w_o = w['b_w_o'][j]
    else:
        n = C_HEADS * C_HEAD_DIM
        w_qkv = w['c_w_qkv'][j]
        q, k, vt = _nbr_proj(x, w_qkv[:, :2 * n].astype(BF16), w_qkv[:, 2 * n:].T.astype(BF16), B, S, bm)
        a = _nbr_attention(q, k, vt, *_nbr_bias(w['c_rpb'][j], S // GRID_W))
        w_o = w['c_w_o'][j]
    return a.reshape(B * S, D_MODEL), w_o.astype(BF16)


def _trunk(groups, w):
    shapes = [x.shape[:2] for x, _ in groups]
    xs = [x.reshape(B * S, D_MODEL) for (x, _), (B, S) in zip(groups, shapes)]
    ps = [p.reshape(p.shape[0], B * S, PLE_DIM) for (_, p), (B, S) in zip(groups, shapes)]
    bms = [min(ROW_BLOCK, S) for _, S in shapes]
    moe_w = {k: _to_bf16(w[k]) for k in ('m_w_gate', 'm_w_up', 'm_w_down')}
    for i in range(DEPTH):
        for g, (B, S) in enumerate(shapes):
            a, w_o = _mixer(i, xs[g], B, S, w, bms[g])
            xs[g] = _out_ln(a, w_o, xs[g], w['ln1_g'][i][None, :], w['ln1_b'][i][None, :], bms[g])
        f_i = i // 2
        wpg, wpi = w['ple_w_gate'][i].astype(BF16), w['ple_w_in'][i].astype(BF16)
        g2, b2 = w['ln2_g'][i][None, :], w['ln2_b'][i][None, :]
        if i % 2 == 0:
            wg, wu, wd = (w[k][f_i].astype(BF16) for k in ('f_w_gate', 'f_w_up', 'f_w_down'))
            for g in range(len(xs)):
                h = _swiglu_up(xs[g], wg, wu, bms[g], FF_DENSE // 2)
                xs[g] = _ffn_down_ln(h, wd, xs[g], ps[g], i, wpg, wpi, g2, b2, bms[g])
        else:
            w_r = jnp.pad(w['m_w_router'][f_i], ((0, 0), (0, 128 - N_EXPERTS)))
            routed = [_router(x, w_r, bm) for x, bm in zip(xs, bms)]
            idx = jnp.concatenate([r[0][:, :TOP_K] for r in routed], axis=0)
            xb = jnp.concatenate([r[2] for r in routed], axis=0)
            slot, row_token, tile_expert, n_tiles = _route(idx, MOE_TILE)
            rows = xb.at[row_token].get(mode="promise_in_bounds")
            ys = _moe_ffn(tile_expert, n_tiles, rows, moe_w['m_w_gate'], moe_w['m_w_up'], moe_w['m_w_down'], f_i, MOE_TILE)
            off = 0
            for g in range(len(xs)):
                n = xs[g].shape[0]
                ya = ys.at[slot[off:off + n, 0]].get(mode="promise_in_bounds")
                yb = ys.at[slot[off:off + n, 1]].get(mode="promise_in_bounds")
                xs[g] = _moe_combine_ln(ya, yb, routed[g][1], xs[g], ps[g], i, wpg, wpi, g2, b2, bms[g])
                off += n
    return tuple(x.reshape(B, S, D_MODEL) for x, (B, S) in zip(xs, shapes))


def kernel(x_prompt, x_sample, p_prompt, p_sample, a_w_dq, a_g_q, a_w_uq, a_w_dkv, a_g_kv, a_w_ukv, a_w_o, b_w_qkv, b_g_q, b_g_k, b_w_o, c_w_qkv, c_rpb, c_w_o, ln1_g, ln1_b, ln2_g, ln2_b, f_w_gate, f_w_up, f_w_down, m_w_router, m_w_gate, m_w_up, m_w_down, ple_w_gate, ple_w_in):
    w = dict(a_w_dq=a_w_dq, a_g_q=a_g_q, a_w_uq=a_w_uq, a_w_dkv=a_w_dkv, a_g_kv=a_g_kv, a_w_ukv=a_w_ukv, a_w_o=a_w_o,
             b_w_qkv=b_w_qkv, b_g_q=b_g_q, b_g_k=b_g_k, b_w_o=b_w_o, c_w_qkv=c_w_qkv, c_rpb=c_rpb, c_w_o=c_w_o,
             ln1_g=ln1_g, ln1_b=ln1_b, ln2_g=ln2_g, ln2_b=ln2_b, f_w_gate=f_w_gate, f_w_up=f_w_up, f_w_down=f_w_down,
             m_w_router=m_w_router, m_w_gate=m_w_gate, m_w_up=m_w_up, m_w_down=m_w_down,
             ple_w_gate=ple_w_gate, ple_w_in=ple_w_in)
    return _trunk([(x_prompt, p_prompt), (x_sample, p_sample)], w)
```
